```python
import jax, jax.numpy as jnp
from jax import lax
import numpy as np

D_MODEL = 2048
BATCH = 1
SEQ = 8192
DEPTH = 1
DEC_BATCH = 128
DEC_SEQ = 4
PAST_LEN = 2048
PAGE_SIZE = 128

GLA_HEADS = 4
GLA_DK = D_MODEL // 2 // GLA_HEADS
GLA_DV = D_MODEL // GLA_HEADS
GLA_KEY_W = GLA_HEADS * GLA_DK
GLA_VAL_W = GLA_HEADS * GLA_DV
GLA_RANK = 16
GLA_TAU = 16.0
GLA_CHUNK = 64
SB_HEAD_DIM = 128
SB_HEADS = D_MODEL // SB_HEAD_DIM
SB_W = SB_HEADS * SB_HEAD_DIM
SB_QBLOCK = 128
SB_BIAS_CENTER = -7.0
NORM_EPS = 1e-6
SPLIT_WIDTHS = (GLA_KEY_W, GLA_KEY_W, GLA_VAL_W, GLA_VAL_W, GLA_RANK,
                SB_W, SB_W, SB_W, SB_W, D_MODEL, D_MODEL)
IN_COLS = sum(SPLIT_WIDTHS)

kernel_name = "hybrid_gla_stickbreaking_gated_merge_step"


def _rmsnorm(x, g):
    xf = x.astype(jnp.float32)
    y = xf * lax.rsqrt(jnp.mean(xf * xf, axis=-1, keepdims=True) + NORM_EPS)
    return (y * g.astype(jnp.float32)).astype(x.dtype)


def _split_cols(u):
    idx = []
    acc = 0
    for w in SPLIT_WIDTHS[:-1]:
        acc += w
        idx.append(acc)
    return jnp.split(u, idx, axis=-1)


def _gla(q, k, v, log_a, s0):
    B, T, H, dk = q.shape
    dv = v.shape[-1]
    C = min(GLA_CHUNK, T)
    n = T // C

    def chunks(a):
        return a.reshape(B, n, C, H, a.shape[-1]).swapaxes(0, 1)

    causal = jnp.tril(jnp.ones((C, C), dtype=bool))[None, :, :, None, None]

    def step(S, inp):
        qc, kc, vc, gc = inp
        b = jnp.cumsum(gc, axis=1)
        diff = jnp.where(causal, b[:, :, None] - b[:, None, :], -jnp.inf)
        decay = jnp.exp(diff)
        attn = jnp.einsum('bthk,bshk,btshk->bhts', qc, kc, decay)
        o = (jnp.einsum('bhts,bshv->bthv', attn, vc)
             + jnp.einsum('bthk,bhkv->bthv', qc * jnp.exp(b), S))
        b_end = b[:, -1]
        S_new = (jnp.exp(b_end)[..., None] * S
                 + jnp.einsum('bshk,bshv->bhkv', kc * jnp.exp(b_end[:, None] - b), vc))
        return S_new, o

    S, o = lax.scan(step, s0, (chunks(q), chunks(k), chunks(v), chunks(log_a)))
    return o.swapaxes(0, 1).reshape(B, T, H, dv), S


def _stick_breaking(q, k, v, bias, q_pos, k_pos):
    B, Tq, H, d = q.shape
    blk = min(SB_QBLOCK, Tq)
    nb = Tq // blk
    qb = q.reshape(B, nb, blk, H, d).swapaxes(0, 1)
    pb = q_pos.reshape(nb, blk)
    inv_sqrt_d = 1.0 / np.sqrt(d).astype(np.float32)
    bias_f = bias.astype(jnp.float32)[None, :, None, None]

    def one_block(args):
        qi, pi = args
        z = jnp.einsum('bqhd,bkhd->bhqk', qi, k).astype(jnp.float32) * inv_sqrt_d + bias_f
        mask = k_pos[None, :] < pi[:, None]
        log_1mb = jnp.where(mask, jax.nn.log_sigmoid(-z), 0.0)
        rev = lax.cumsum(log_1mb, axis=3, reverse=True) - log_1mb
        A = jnp.where(mask, jnp.exp(jax.nn.log_sigmoid(z) + rev), 0.0)
        return jnp.einsum('bhqk,bkhd->bqhd', A.astype(v.dtype), v)

    out = lax.map(one_block, (qb, pb))
    return out.swapaxes(0, 1).reshape(B, Tq, H, d)


def _layer(x, c, w, s0, past_k, past_v):
    (w_ada, b_ada, norm_g, w_in, w_gate_up, b_gate, gla_norm_g,
     q_norm_g, k_norm_g, sb_bias, w_branch_a, w_branch_b, w_out) = w
    B, T, _ = x.shape
    mod = jax.nn.silu(c) @ w_ada + b_ada
    shift, scale, gate = jnp.split(mod, 3, axis=-1)
    h = _rmsnorm(x, norm_g) * (1 + scale[:, None]) + shift[:, None]
    qa, ka, va, za, glr, qb, kb, vb, zb, ga, gb = _split_cols(h @ w_in)

    f32 = jnp.float32
    qa = qa.reshape(B, T, GLA_HEADS, GLA_DK).astype(f32) * (GLA_DK ** -0.5)
    ka = ka.reshape(B, T, GLA_HEADS, GLA_DK).astype(f32)
    va = va.reshape(B, T, GLA_HEADS, GLA_DV).astype(f32)
    log_a = (jax.nn.log_sigmoid((glr @ w_gate_up + b_gate).astype(f32)) / GLA_TAU
             ).reshape(B, T, GLA_HEADS, GLA_DK)
    oa, s_new = _gla(qa, ka, va, log_a, s0.astype(f32))
    oa = _rmsnorm(oa, gla_norm_g).astype(x.dtype).reshape(B, T, GLA_VAL_W) * jax.nn.silu(za)

    qb = _rmsnorm(qb.reshape(B, T, SB_HEADS, SB_HEAD_DIM), q_norm_g)
    kb = _rmsnorm(kb.reshape(B, T, SB_HEADS, SB_HEAD_DIM), k_norm_g)
    vb = vb.reshape(B, T, SB_HEADS, SB_HEAD_DIM)
    if past_k is None:
        pos0 = 0
        k_all, v_all = kb, vb
    else:
        pos0 = past_k.shape[1]
        k_all = jnp.concatenate([past_k.astype(kb.dtype), kb], axis=1)
        v_all = jnp.concatenate([past_v.astype(vb.dtype), vb], axis=1)
    q_pos = pos0 + jnp.arange(T, dtype=jnp.int32)
    k_pos = jnp.arange(k_all.shape[1], dtype=jnp.int32)
    ob = _stick_breaking(qb, k_all, v_all, sb_bias, q_pos, k_pos).reshape(B, T, SB_W) * jax.nn.silu(zb)

    y = jax.nn.sigmoid(ga) * (oa @ w_branch_a) + jax.nn.sigmoid(gb) * (ob @ w_branch_b)
    out = x + gate[:, None] * (y @ w_out)
    return out, kb, vb, s_new.astype(x.dtype)


def setup_inputs(seed: int = 0) -> dict:
    key = jax.random.key(seed)
    ks = jax.random.split(key, 21)
    n_pages = PAST_LEN // PAGE_SIZE
    n_used = DEC_BATCH * n_pages
    n_phys = (5 * n_used + 3) // 4
    f = jnp.float32
    nrm = lambda k, shape, s: jax.random.normal(k, shape, f) * s
    perm = jax.random.permutation(ks[5], n_phys)
    page_table = perm[:n_used].reshape(DEC_BATCH, n_pages).astype(jnp.int32)
    D = D_MODEL
    return {
        "x_prompt": nrm(ks[0], (BATCH, SEQ, D), 1.0),
        "x_sample": nrm(ks[1], (DEC_BATCH, DEC_SEQ, D), 1.0),
        "cache_k": nrm(ks[2], (DEPTH, n_phys, PAGE_SIZE, SB_HEADS, SB_HEAD_DIM), 1.0),
        "cache_v": nrm(ks[3], (DEPTH, n_phys, PAGE_SIZE, SB_HEADS, SB_HEAD_DIM), 1.0),
        "state_gla": nrm(ks[4], (DEPTH, DEC_BATCH, GLA_HEADS, GLA_DK, GLA_DV), 1.0),
        "page_table": page_table,
        "c_prompt": nrm(ks[6], (BATCH, D), 1.0),
        "c_sample": nrm(ks[7], (DEC_BATCH, D), 1.0),
        "w_ada": nrm(ks[8], (DEPTH, D, 3 * D), 0.5 * D ** -0.5),
        "b_ada": nrm(ks[9], (DEPTH, 3 * D), 0.02),
        "norm_g": 1.0 + nrm(ks[10], (DEPTH, D), 0.1),
        "w_in": nrm(ks[11], (DEPTH, D, IN_COLS), D ** -0.5),
        "w_gate_up": nrm(ks[12], (DEPTH, GLA_RANK, GLA_KEY_W), GLA_RANK ** -0.5),
        "b_gate": nrm(ks[13], (DEPTH, GLA_KEY_W), 0.1),
        "gla_norm_g": 1.0 + nrm(ks[14], (DEPTH, GLA_DV), 0.1),
        "q_norm_g": 1.0 + nrm(ks[15], (DEPTH, SB_HEAD_DIM), 0.1),
        "k_norm_g": 1.0 + nrm(ks[16], (DEPTH, SB_HEAD_DIM), 0.1),
        "sb_bias": SB_BIAS_CENTER + nrm(ks[20], (DEPTH, SB_HEADS), 0.5),
        "w_branch_a": nrm(ks[17], (DEPTH, GLA_VAL_W, D), GLA_VAL_W ** -0.5),
        "w_branch_b": nrm(ks[18], (DEPTH, SB_W, D), SB_W ** -0.5),
        "w_out": nrm(ks[19], (DEPTH, D, D), D ** -0.5),
    }


def reference(x_prompt, x_sample, cache_k, cache_v, state_gla, page_table, c_prompt, c_sample,
              w_ada, b_ada, norm_g, w_in, w_gate_up, b_gate, gla_norm_g, q_norm_g, k_norm_g,
              sb_bias, w_branch_a, w_branch_b, w_out):
    xp, xs = x_prompt, x_sample
    B = xp.shape[0]
    DB = xs.shape[0]
    kp_l, vp_l, sp_l, ks_l, vs_l, ss_l = [], [], [], [], [], []
    for l in range(DEPTH):
        w = (w_ada[l], b_ada[l], norm_g[l], w_in[l], w_gate_up[l], b_gate[l], gla_norm_g[l],
             q_norm_g[l], k_norm_g[l], sb_bias[l], w_branch_a[l], w_branch_b[l], w_out[l])
        s0 = jnp.zeros((B, GLA_HEADS, GLA_DK, GLA_DV), jnp.float32)
        xp, kp, vp, sp = _layer(xp, c_prompt, w, s0, None, None)
        past_k = cache_k[l][page_table].reshape(DB, -1, SB_HEADS, SB_HEAD_DIM)
        past_v = cache_v[l][page_table].reshape(DB, -1, SB_HEADS, SB_HEAD_DIM)
        xs, ks_, vs_, ss = _layer(xs, c_sample, w, state_gla[l], past_k, past_v)
        kp_l.append(kp); vp_l.append(vp); sp_l.append(sp)
        ks_l.append(ks_); vs_l.append(vs_); ss_l.append(ss)
    k_prompt = jnp.stack(kp_l)
    v_prompt = jnp.stack(vp_l)
    state_gla_prompt = jnp.stack(sp_l)
    k_sample = jnp.stack(ks_l)
    v_sample = jnp.stack(vs_l)
    state_gla_sample = jnp.stack(ss_l)
    return (xp, xs, k_prompt, v_prompt, state_gla_prompt, k_sample, v_sample, state_gla_sample)
```

```python
import functools
import math

import numpy as np
import jax
import jax.numpy as jnp
from jax import lax
from jax.experimental import pallas as pl
from jax.experimental.pallas import tpu as pltpu

F32 = jnp.float32
BF16 = jnp.bfloat16

GLA_HEADS = 4
GLA_RANK = 16
GLA_TAU = 16.0
SB_HEAD_DIM = 128
NORM_EPS = 1e-6
LOG2E = 1.4426950408889634
LN2 = 0.6931471805599453

LANES = 128
VMEM_LIMIT = 56 * 1024 * 1024


def _cparams(sem):
    return pltpu.CompilerParams(dimension_semantics=sem, vmem_limit_bytes=VMEM_LIMIT)


def _sigmoid(x):
    return 1.0 / (1.0 + jnp.exp(-x))


def _softplus_parts(z):
    l = jnp.log(1.0 + jnp.exp(-jnp.abs(z)))
    return jnp.maximum(z, 0.0) + l, l


def _ada_kernel(c_ref, w_ref, b_ref, o_ref):
    c = c_ref[...]
    s = (c * _sigmoid(c)).astype(BF16)
    o_ref[...] = jnp.dot(s, w_ref[...].astype(BF16), preferred_element_type=F32) + b_ref[...]


def _ada(c_all, w_ada, b_ada, tn=512):
    r, d = c_all.shape
    n = w_ada.shape[1]
    return pl.pallas_call(
        _ada_kernel,
        out_shape=jax.ShapeDtypeStruct((r, n), F32),
        grid=(n // tn,),
        in_specs=[pl.BlockSpec((r, d), lambda j: (0, 0)),
                  pl.BlockSpec((d, tn), lambda j: (0, j)),
                  pl.BlockSpec((1, tn), lambda j: (0, j))],
        out_specs=pl.BlockSpec((r, tn), lambda j: (0, j)),
        compiler_params=_cparams(("arbitrary",)),
        name="ada_mod",
    )(c_all, w_ada, b_ada.reshape(1, n))


def _norm_kernel(x_ref, g_ref, scale_ref, shift_ref, h_ref):
    x = x_ref[...]
    ms = jnp.mean(x * x, axis=-1, keepdims=True)
    y = x * lax.rsqrt(ms + NORM_EPS) * g_ref[...]
    h_ref[...] = (y * (1.0 + scale_ref[...]) + shift_ref[...]).astype(h_ref.dtype)


def _mod_norm(x, g, scale, shift, tm):
    m, d = x.shape
    per_row = scale.shape[0] != 1
    mod_spec = (pl.BlockSpec((tm, d), lambda i: (i, 0)) if per_row
                else pl.BlockSpec((1, d), lambda i: (0, 0)))
    return pl.pallas_call(
        _norm_kernel,
        out_shape=jax.ShapeDtypeStruct((m, d), BF16),
        grid=(m // tm,),
        in_specs=[pl.BlockSpec((tm, d), lambda i: (i, 0)),
                  pl.BlockSpec((1, d), lambda i: (0, 0)),
                  mod_spec, mod_spec],
        out_specs=pl.BlockSpec((tm, d), lambda i: (i, 0)),
        compiler_params=_cparams(("arbitrary",)),
        name="mod_norm",
    )(x, g.reshape(1, d), scale, shift)


def _proj_kernel(h_ref, w_ref, o_ref, *, scale):
    acc = jnp.dot(h_ref[...], w_ref[...], preferred_element_type=F32)
    if scale != 1.0:
        acc = acc * scale
    o_ref[...] = acc.astype(o_ref.dtype)


def _proj(h, w, out_dtype, tm, tn, scale=1.0, name="proj"):
    m, k = h.shape
    n = w.shape[1]
    tn = min(tn, n)
    return pl.pallas_call(
        functools.partial(_proj_kernel, scale=scale),
        out_shape=jax.ShapeDtypeStruct((m, n), out_dtype),
        grid=(m // tm, n // tn),
        in_specs=[pl.BlockSpec((tm, k), lambda i, j: (i, 0)),
                  pl.BlockSpec((k, tn), lambda i, j: (0, j))],
        out_specs=pl.BlockSpec((tm, tn), lambda i, j: (i, j)),
        compiler_params=_cparams(("arbitrary", "arbitrary")),
        name=name,
    )(h, w)


def _proj_headnorm_kernel(h_ref, w_ref, g_ref, *o_refs, scale):
    acc = jnp.dot(h_ref[...], w_ref[...], preferred_element_type=F32)
    g = g_ref[...]
    tn = acc.shape[1]
    for c in range(tn // SB_HEAD_DIM):
        cols = slice(c * SB_HEAD_DIM, (c + 1) * SB_HEAD_DIM)
        blk = acc[:, cols]
        ms = jnp.mean(blk * blk, axis=-1, keepdims=True)
        y = blk * lax.rsqrt(ms + NORM_EPS) * g
        for o_ref in o_refs:
            if o_ref.dtype == F32:
                o_ref[:, cols] = y
            else:
                o_ref[:, cols] = (y * scale).astype(o_ref.dtype)


def _proj_headnorm(h, w, g, tm, tn, out_dtypes, scale, name):
    m, k = h.shape
    n = w.shape[1]
    tile = pl.BlockSpec((tm, tn), lambda i, j: (i, j))
    return pl.pallas_call(
        functools.partial(_proj_headnorm_kernel, scale=scale),
        out_shape=tuple(jax.ShapeDtypeStruct((m, n), dt) for dt in out_dtypes),
        grid=(m // tm, n // tn),
        in_specs=[pl.BlockSpec((tm, k), lambda i, j: (i, 0)),
                  pl.BlockSpec((k, tn), lambda i, j: (0, j)),
                  pl.BlockSpec((1, SB_HEAD_DIM), lambda i, j: (0, 0))],
        out_specs=tuple(tile for _ in out_dtypes),
        compiler_params=_cparams(("arbitrary", "arbitrary")),
        name=name,
    )(h, w, g.reshape(1, SB_HEAD_DIM))


def _proj_dual_kernel(h_ref, w_ref, o32_ref, o16_ref):
    acc = jnp.dot(h_ref[...], w_ref[...], preferred_element_type=F32)
    o32_ref[...] = acc
    o16_ref[...] = acc.astype(BF16)


def _proj_dual(h, w, tm, tn, name):
    m, k = h.shape
    n = w.shape[1]
    return pl.pallas_call(
        _proj_dual_kernel,
        out_shape=(jax.ShapeDtypeStruct((m, n), F32), jax.ShapeDtypeStruct((m, n), BF16)),
        grid=(m // tm, n // tn),
        in_specs=[pl.BlockSpec((tm, k), lambda i, j: (i, 0)),
                  pl.BlockSpec((k, tn), lambda i, j: (0, j))],
        out_specs=(pl.BlockSpec((tm, tn), lambda i, j: (i, j)),
                   pl.BlockSpec((tm, tn), lambda i, j: (i, j))),
        compiler_params=_cparams(("arbitrary", "arbitrary")),
        name=name,
    )(h, w)


def _merge_kernel(oa_ref, ob_ref, wa_ref, wb_ref, ga_ref, gb_ref, y_ref):
    ya = jnp.dot(oa_ref[...], wa_ref[...], preferred_element_type=F32)
    yb = jnp.dot(ob_ref[...], wb_ref[...], preferred_element_type=F32)
    y_ref[...] = (_sigmoid(ga_ref[...]) * ya + _sigmoid(gb_ref[...]) * yb).astype(y_ref.dtype)


def _merge(oa, ob, wa, wb, uc, ga_col, gb_col, tm, tn):
    m, k = oa.shape
    n = wa.shape[1]
    row = lambda i, j: (i, 0)
    col = lambda i, j: (0, j)
    ja, jb = ga_col // tn, gb_col // tn
    return pl.pallas_call(
        _merge_kernel,
        out_shape=jax.ShapeDtypeStruct((m, n), BF16),
        grid=(m // tm, n // tn),
        in_specs=[pl.BlockSpec((tm, k), row), pl.BlockSpec((tm, k), row),
                  pl.BlockSpec((k, tn), col), pl.BlockSpec((k, tn), col),
                  pl.BlockSpec((tm, tn), lambda i, j: (i, ja + j)),
                  pl.BlockSpec((tm, tn), lambda i, j: (i, jb + j))],
        out_specs=pl.BlockSpec((tm, tn), lambda i, j: (i, j)),
        compiler_params=_cparams(("arbitrary", "arbitrary")),
        name="merge",
    )(oa, ob, wa, wb, uc, uc)


def _out_kernel(y_ref, w_ref, x_ref, gate_ref, o_ref):
    acc = jnp.dot(y_ref[...], w_ref[...], preferred_element_type=F32)
    o_ref[...] = x_ref[...] + gate_ref[...] * acc


def _out_proj(y, w, x, gate, tm, tn):
    m, k = y.shape
    n = w.shape[1]
    per_row = gate.shape[0] != 1
    gate_spec = (pl.BlockSpec((tm, tn), lambda i, j: (i, j)) if per_row
                 else pl.BlockSpec((1, tn), lambda i, j: (0, j)))
    return pl.pallas_call(
        _out_kernel,
        out_shape=jax.ShapeDtypeStruct((m, n), F32),
        grid=(m // tm, n // tn),
        in_specs=[pl.BlockSpec((tm, k), lambda i, j: (i, 0)),
                  pl.BlockSpec((k, tn), lambda i, j: (0, j)),
                  pl.BlockSpec((tm, tn), lambda i, j: (i, j)),
                  gate_spec],
        out_specs=pl.BlockSpec((tm, tn), lambda i, j: (i, j)),
        compiler_params=_cparams(("arbitrary", "arbitrary")),
        name="out_proj",
    )(y, w, x, gate)


GLA_SUB = 16
GLA_ROWS = 256


def _split3(x):
    hi = x.astype(BF16)
    r1 = x - hi.astype(F32)
    mid = r1.astype(BF16)
    lo = (r1 - mid.astype(F32)).astype(BF16)
    return hi, mid, lo


def _gla_prompt_kernel(q_ref, k_ref, v_ref, z_ref, glr_ref, wg_ref, bg_ref, gn_ref, tri_ref,
                       o_ref, s_ref, st_ref, b_ref):
    t = pl.program_id(1)
    nt = pl.num_programs(1)
    rows, dk = q_ref.shape
    dv = v_ref.shape[1]
    c = GLA_SUB

    @pl.when(t == 0)
    def _():
        st_ref[...] = jnp.zeros_like(st_ref)

    x = jnp.dot(glr_ref[...].astype(BF16), wg_ref[...], preferred_element_type=F32) + bg_ref[...]
    la = (jnp.minimum(x, 0.0) - jnp.log(1.0 + jnp.exp(-jnp.abs(x)))) * (1.0 / GLA_TAU)
    tri = tri_ref[...]
    hi, mid, lo = _split3(la)
    b_ref[...] = (jnp.dot(tri, hi, preferred_element_type=F32)
                  + jnp.dot(tri, mid, preferred_element_type=F32)
                  + jnp.dot(tri, lo, preferred_element_type=F32))

    row_id = lax.broadcasted_iota(jnp.int32, (c, 1), 0)
    gn = gn_ref[...]

    def chunk(ci, carry):
        r0 = pl.multiple_of(ci * c, c)
        qc = q_ref[pl.ds(r0, c), :]
        kc = k_ref[pl.ds(r0, c), :]
        vc = v_ref[pl.ds(r0, c), :]
        bc = b_ref[pl.ds(r0, c), :]
        b_end = b_ref[pl.ds(r0 + c - 1, 1), :]
        st = st_ref[...]
        qd = (qc * jnp.exp(bc)).astype(BF16)
        o = lax.dot_general(qd, st.astype(BF16), (((1,), (1,)), ((), ())),
                            preferred_element_type=F32)
        for s in range(c):
            ks = k_ref[pl.ds(r0 + s, 1), :]
            bs = b_ref[pl.ds(r0 + s, 1), :]
            vs = v_ref[pl.ds(r0 + s, 1), :]
            w = jnp.where(row_id >= s, qc * ks * jnp.exp(bc - bs), 0.0)
            o = o + jnp.sum(w, axis=-1, keepdims=True) * vs
        ms = jnp.mean(o * o, axis=-1, keepdims=True)
        zc = z_ref[pl.ds(r0, c), :]
        y = o * lax.rsqrt(ms + NORM_EPS) * gn * (zc * _sigmoid(zc))
        o_ref[pl.ds(r0, c), :] = y.astype(o_ref.dtype)
        kd = (kc * jnp.exp(b_end - bc)).astype(BF16)
        upd = lax.dot_general(vc.astype(BF16), kd, (((0,), (0,)), ((), ())),
                              preferred_element_type=F32)
        st_ref[...] = st * jnp.exp(b_end) + upd
        return carry

    lax.fori_loop(0, rows // c, chunk, 0)

    @pl.when(t == nt - 1)
    def _():
        s_ref[0] = st_ref[...].T


def _gla_prompt(ua, glr, wg, bg, gn, dk, dv):
    t_len = ua.shape[0]
    h = GLA_HEADS
    rows = min(GLA_ROWS, t_len)
    r = np.arange(rows)
    tri = ((r[:, None] >= r[None, :]) & (r[:, None] // GLA_SUB == r[None, :] // GLA_SUB))
    tri = jnp.asarray(tri, dtype=BF16)
    k_blk = h
    v_blk = 2 * h * dk // dv
    z_blk = v_blk + h
    return pl.pallas_call(
        _gla_prompt_kernel,
        out_shape=(jax.ShapeDtypeStruct((t_len, h * dv), BF16),
                   jax.ShapeDtypeStruct((h, dk, dv), F32)),
        grid=(h, t_len // rows),
        in_specs=[pl.BlockSpec((rows, dk), lambda hh, tt: (tt, hh)),
                  pl.BlockSpec((rows, dk), lambda hh, tt: (tt, k_blk + hh)),
                  pl.BlockSpec((rows, dv), lambda hh, tt: (tt, v_blk + hh)),
                  pl.BlockSpec((rows, dv), lambda hh, tt: (tt, z_blk + hh)),
                  pl.BlockSpec((rows, GLA_RANK), lambda hh, tt: (tt, 0)),
                  pl.BlockSpec((GLA_RANK, dk), lambda hh, tt: (0, hh)),
                  pl.BlockSpec((1, dk), lambda hh, tt: (0, hh)),
                  pl.BlockSpec((1, dv), lambda hh, tt: (0, 0)),
                  pl.BlockSpec((rows, rows), lambda hh, tt: (0, 0))],
        out_specs=(pl.BlockSpec((rows, dv), lambda hh, tt: (tt, hh)),
                   pl.BlockSpec((1, dk, dv), lambda hh, tt: (hh, 0, 0))),
        scratch_shapes=[pltpu.VMEM((dv, dk), F32), pltpu.VMEM((rows, dk), F32)],
        compiler_params=_cparams(("arbitrary", "arbitrary")),
        name="gla_prompt",
    )(ua, ua, ua, ua, glr, wg, bg.reshape(1, -1), gn.reshape(1, -1), tri)


GLA_PAD_ROWS = 8


def _gla_sample_kernel(q_ref, k_ref, v_ref, z_ref, glr_ref, wg_ref, bg_ref, gn_ref, s_ref,
                       o_ref, so_ref, *, n_tok):
    rows = q_ref.shape[1]
    dk = s_ref.shape[2]
    dv = s_ref.shape[3]
    row_id = lax.broadcasted_iota(jnp.int32, (rows, 1), 0)
    gn = gn_ref[...]
    glr = glr_ref[0].astype(BF16)
    for h in range(GLA_HEADS):
        ksl = slice(h * dk, (h + 1) * dk)
        vsl = slice(h * dv, (h + 1) * dv)
        x = jnp.dot(glr, wg_ref[:, ksl], preferred_element_type=F32) + bg_ref[:, ksl]
        la = (jnp.minimum(x, 0.0) - jnp.log(1.0 + jnp.exp(-jnp.abs(x)))) * (1.0 / GLA_TAU)
        bc = jnp.zeros_like(la)
        for s in range(n_tok):
            bc = bc + jnp.where(row_id >= s, la[s:s + 1, :], 0.0)
        b_end = bc[n_tok - 1:n_tok, :]
        qc = q_ref[0, :, ksl]
        kc = k_ref[0, :, ksl]
        vc = v_ref[0, :, vsl]
        st = s_ref[0, h].T
        qd = (qc * jnp.exp(bc)).astype(BF16)
        o = lax.dot_general(qd, st.astype(BF16), (((1,), (1,)), ((), ())),
                            preferred_element_type=F32)
        for s in range(n_tok):
            w = jnp.where(row_id >= s, qc * kc[s:s + 1, :] * jnp.exp(bc - bc[s:s + 1, :]), 0.0)
            o = o + jnp.sum(w, axis=-1, keepdims=True) * vc[s:s + 1, :]
        ms = jnp.mean(o * o, axis=-1, keepdims=True)
        zc = z_ref[0, :, vsl]
        o_ref[0, :, vsl] = o * lax.rsqrt(ms + NORM_EPS) * gn * (zc * _sigmoid(zc))
        kd = (kc * jnp.exp(b_end - bc)).astype(BF16)
        upd = lax.dot_general(vc.astype(BF16), kd, (((0,), (0,)), ((), ())),
                              preferred_element_type=F32)
        so_ref[0, h] = (st * jnp.exp(b_end) + upd).T


def _gla_sample(ua, glr, wg, bg, gn, state, n_tok):
    b, rows, _ = ua.shape
    h, dk, dv = state.shape[1:]
    k_blk = h * dk // (h * dk)
    v_blk = 2 * h * dk // (h * dv)
    z_blk = v_blk + 1
    return pl.pallas_call(
        functools.partial(_gla_sample_kernel, n_tok=n_tok),
        out_shape=(jax.ShapeDtypeStruct((b, rows, h * dv), F32),
                   jax.ShapeDtypeStruct(state.shape, F32)),
        grid=(b,),
        in_specs=[pl.BlockSpec((1, rows, h * dk), lambda i: (i, 0, 0)),
                  pl.BlockSpec((1, rows, h * dk), lambda i: (i, 0, k_blk)),
                  pl.BlockSpec((1, rows, h * dv), lambda i: (i, 0, v_blk)),
                  pl.BlockSpec((1, rows, h * dv), lambda i: (i, 0, z_blk)),
                  pl.BlockSpec((1, rows, GLA_RANK), lambda i: (i, 0, 0)),
                  pl.BlockSpec((GLA_RANK, h * dk), lambda i: (0, 0)),
                  pl.BlockSpec((1, h * dk), lambda i: (0, 0)),
                  pl.BlockSpec((1, dv), lambda i: (0, 0)),
                  pl.BlockSpec((1, h, dk, dv), lambda i: (i, 0, 0, 0))],
        out_specs=(pl.BlockSpec((1, rows, h * dv), lambda i: (i, 0, 0)),
                   pl.BlockSpec((1, h, dk, dv), lambda i: (i, 0, 0, 0))),
        compiler_params=_cparams(("arbitrary",)),
        name="gla_sample",
    )(ua, ua, ua, ua, glr, wg, bg.reshape(1, -1), gn.reshape(1, -1), state)


SB_BLK = 512
SB_SUB = 128


def _suffix_matrix():
    j = np.arange(SB_SUB)
    u = (j[:, None] > j[None, :]).astype(np.float32)
    return jnp.asarray(np.concatenate([u, np.ones((SB_SUB, SB_SUB), np.float32)], axis=1), dtype=BF16)


def _sb_tile(z, uw, car, mask):
    l = jnp.log(1.0 + jnp.exp(-jnp.abs(z)))
    sp = jnp.maximum(z, 0.0) + l
    logb = jnp.minimum(z, 0.0) - l
    if mask is not None:
        sp = jnp.where(mask, sp, 0.0)
    hi = sp.astype(BF16)
    mid = (sp - hi.astype(F32)).astype(BF16)
    s2 = (jnp.dot(hi, uw, preferred_element_type=F32)
          + jnp.dot(mid, uw, preferred_element_type=F32))
    a = jnp.exp(logb - s2[:, :SB_SUB] - car)
    if mask is not None:
        a = jnp.where(mask, a, 0.0)
    return a, car + s2[:, SB_SUB:]


def _sb_prompt_kernel(qi_ref, kj_ref, q_ref, k_ref, v_ref, z_ref, bias_ref, uw_ref,
                      o_ref, acc_ref, car_ref):
    step = pl.program_id(1)
    qi = qi_ref[step]
    kj = kj_ref[step]
    nsub = q_ref.shape[0] // SB_SUB
    uw = uw_ref[...]
    bias = bias_ref[0]

    @pl.when(kj == qi)
    def _():
        acc_ref[...] = jnp.zeros_like(acc_ref)
        car_ref[...] = jnp.zeros_like(car_ref)

    def subtile(qs, ks, masked):
        rq = slice(qs * SB_SUB, (qs + 1) * SB_SUB)
        rk = slice(ks * SB_SUB, (ks + 1) * SB_SUB)
        z = lax.dot_general(q_ref[rq, :], k_ref[rk, :], (((1,), (1,)), ((), ())),
                            preferred_element_type=F32) + bias
        mask = None
        if masked:
            mask = (lax.broadcasted_iota(jnp.int32, z.shape, 1)
                    < lax.broadcasted_iota(jnp.int32, z.shape, 0))
        a, car = _sb_tile(z, uw, car_ref[rq, :], mask)
        car_ref[rq, :] = car
        acc_ref[rq, :] += jnp.dot(a.astype(BF16), v_ref[rk, :], preferred_element_type=F32)

    @pl.when(kj == qi)
    def _():
        for qs in range(nsub):
            for ks in range(qs, -1, -1):
                subtile(qs, ks, ks == qs)

    @pl.when(kj != qi)
    def _():
        for qs in range(nsub):
            for ks in range(nsub - 1, -1, -1):
                subtile(qs, ks, False)

    @pl.when(kj == 0)
    def _():
        zz = z_ref[...]
        o_ref[...] = (acc_ref[...] * (zz * _sigmoid(zz))).astype(o_ref.dtype)


def _sb_prompt(q16, k16, v16, uc, bias):
    t_len, w = q16.shape
    heads = w // SB_HEAD_DIM
    blk = min(SB_BLK, t_len)
    nq = t_len // blk
    qi = np.concatenate([np.full(i + 1, i) for i in range(nq)]).astype(np.int32)
    kj = np.concatenate([np.arange(i, -1, -1) for i in range(nq)]).astype(np.int32)
    bias_b = jnp.broadcast_to(bias.astype(F32)[:, None, None], (heads, 1, SB_HEAD_DIM))
    qmap = lambda h, s, qi_r, kj_r: (qi_r[s], h)
    kmap = lambda h, s, qi_r, kj_r: (kj_r[s], h)
    return pl.pallas_call(
        _sb_prompt_kernel,
        out_shape=jax.ShapeDtypeStruct((t_len, w), BF16),
        grid_spec=pltpu.PrefetchScalarGridSpec(
            num_scalar_prefetch=2,
            grid=(heads, len(qi)),
            in_specs=[pl.BlockSpec((blk, SB_HEAD_DIM), qmap),
                      pl.BlockSpec((blk, SB_HEAD_DIM), kmap),
                      pl.BlockSpec((blk, SB_HEAD_DIM), kmap),
                      pl.BlockSpec((blk, SB_HEAD_DIM), qmap),
                      pl.BlockSpec((1, 1, SB_HEAD_DIM), lambda h, s, qi_r, kj_r: (h, 0, 0)),
                      pl.BlockSpec((SB_SUB, 2 * SB_SUB), lambda h, s, qi_r, kj_r: (0, 0))],
            out_specs=pl.BlockSpec((blk, SB_HEAD_DIM), qmap),
            scratch_shapes=[pltpu.VMEM((blk, SB_HEAD_DIM), F32),
                            pltpu.VMEM((blk, SB_HEAD_DIM), F32)]),
        compiler_params=_cparams(("arbitrary", "arbitrary")),
        name="sb_prompt",
    )(jnp.asarray(qi), jnp.asarray(kj), q16, k16, v16, uc, bias_b, _suffix_matrix())


SB_GROUP = 2
SB_GW = SB_GROUP * SB_HEAD_DIM


def _sb_sample_kernel(pt_ref, q_ref, kc_ref, vc_ref, kn_ref, vn_ref, z_ref, bias_ref, uw_ref,
                      o_ref, acc_ref, car_ref, kpg_ref, vpg_ref, *, n_tok):
    b = pl.program_id(0)
    p = pl.program_id(1)
    n_pages = pl.num_programs(1)
    ngrp = q_ref.shape[1]
    rows = q_ref.shape[2]
    uw = uw_ref[...]
    bias = bias_ref[...]

    def page(k_page_ref, v_page_ref, masked):
        zs = []
        for g in range(ngrp):
            kg = k_page_ref[:, g * SB_GW:(g + 1) * SB_GW].astype(BF16)
            zs.append(lax.dot_general(q_ref[0, g], kg, (((1,), (1,)), ((), ())),
                                      preferred_element_type=F32))
        z = jnp.concatenate(zs, axis=0) + bias
        mask = None
        if masked:
            tok = lax.broadcasted_iota(jnp.int32, z.shape, 0) % n_tok
            mask = lax.broadcasted_iota(jnp.int32, z.shape, 1) < tok
        a, car = _sb_tile(z, uw, car_ref[...], mask)
        car_ref[...] = car
        a16 = a.astype(BF16)
        for g in range(ngrp):
            vg = v_page_ref[:, g * SB_GW:(g + 1) * SB_GW].astype(BF16)
            acc_ref[g * rows:(g + 1) * rows, :] += jnp.dot(
                a16[g * rows:(g + 1) * rows, :], vg, preferred_element_type=F32)

    @pl.when(jnp.logical_and(b == 0, p == 0))
    def _():
        kpg_ref[...] = jnp.zeros_like(kpg_ref)
        vpg_ref[...] = jnp.zeros_like(vpg_ref)

    @pl.when(p == 0)
    def _():
        acc_ref[...] = jnp.zeros_like(acc_ref)
        car_ref[...] = jnp.zeros_like(car_ref)
        kpg_ref[0:kn_ref.shape[1], :] = kn_ref[0]
        vpg_ref[0:vn_ref.shape[1], :] = vn_ref[0]
        page(kpg_ref, vpg_ref, True)

    page(kc_ref.at[0], vc_ref.at[0], False)

    @pl.when(p == n_pages - 1)
    def _():
        sel = lax.broadcasted_iota(jnp.int32, (rows, SB_HEAD_DIM), 0) < n_tok
        for g in range(ngrp):
            blk = acc_ref[g * rows:(g + 1) * rows, :]
            og = jnp.where(sel, blk[:, :SB_HEAD_DIM], blk[:, SB_HEAD_DIM:])
            zz = z_ref[0, g]
            o_ref[0, g] = og * (zz * _sigmoid(zz))


def _sb_sample(page_table, qbd, cache_k, cache_v, k_new, v_new, z_g, bias_rows, n_tok):
    b, ngrp, rows, _ = qbd.shape
    n_pages = page_table.shape[1]
    page = cache_k.shape[1]
    w = cache_k.shape[2]
    cmap = lambda i, p, pt: (pt[i, n_pages - 1 - p], 0, 0)
    smap3 = lambda i, p, pt: (i, 0, 0)
    smap4 = lambda i, p, pt: (i, 0, 0, 0)
    return pl.pallas_call(
        functools.partial(_sb_sample_kernel, n_tok=n_tok),
        out_shape=jax.ShapeDtypeStruct((b, ngrp, rows, SB_HEAD_DIM), F32),
        grid_spec=pltpu.PrefetchScalarGridSpec(
            num_scalar_prefetch=1,
            grid=(b, n_pages),
            in_specs=[pl.BlockSpec((1, ngrp, rows, SB_GW), smap4),
                      pl.BlockSpec((1, page, w), cmap),
                      pl.BlockSpec((1, page, w), cmap),
                      pl.BlockSpec((1, k_new.shape[1], w), smap3),
                      pl.BlockSpec((1, v_new.shape[1], w), smap3),
                      pl.BlockSpec((1, ngrp, rows, SB_HEAD_DIM), smap4),
                      pl.BlockSpec((ngrp * rows, SB_HEAD_DIM), lambda i, p, pt: (0, 0)),
                      pl.BlockSpec((SB_SUB, 2 * SB_SUB), lambda i, p, pt: (0, 0))],
            out_specs=pl.BlockSpec((1, ngrp, rows, SB_HEAD_DIM), smap4),
            scratch_shapes=[pltpu.VMEM((ngrp * rows, SB_GW), F32),
                            pltpu.VMEM((ngrp * rows, SB_HEAD_DIM), F32),
                            pltpu.VMEM((page, w), F32),
                            pltpu.VMEM((page, w), F32)]),
        compiler_params=_cparams(("arbitrary", "arbitrary")),
        name="sb_sample",
    )(page_table, qbd, cache_k, cache_v, k_new, v_new, z_g, bias_rows, _suffix_matrix())


def _pad_rows(a, rows):
    return jnp.pad(a, ((0, 0), (0, rows - a.shape[1]), (0, 0)))


def _layer(xp, xs, c_all, w, cache_k, cache_v, state, page_table, n_seq, n_tok):
    (w_ada, b_ada, norm_g, w_in, w_gate_up, b_gate, gla_norm_g,
     q_norm_g, k_norm_g, sb_bias, w_branch_a, w_branch_b, w_out) = w
    d = xp.shape[1]
    heads = d // SB_HEAD_DIM
    dv = gla_norm_g.shape[0]
    key_w = w_gate_up.shape[1]
    dk = key_w // GLA_HEADS
    val_w = GLA_HEADS * dv
    sb_w = heads * SB_HEAD_DIM

    c0 = 2 * key_w + 2 * val_w
    c1 = c0 + GLA_RANK
    w_a = jnp.concatenate([w_in[:, :key_w] * (dk ** -0.5), w_in[:, key_w:c0]], axis=1).astype(BF16)
    w_glr = w_in[:, c0:c1].astype(BF16)
    w_qb = w_in[:, c1:c1 + sb_w].astype(BF16)
    w_kb = w_in[:, c1 + sb_w:c1 + 2 * sb_w].astype(BF16)
    w_vb = w_in[:, c1 + 2 * sb_w:c1 + 3 * sb_w].astype(BF16)
    w_c = w_in[:, c1 + 3 * sb_w:].astype(BF16)
    wa16, wb16, wo16 = (w_branch_a.astype(BF16), w_branch_b.astype(BF16), w_out.astype(BF16))
    wg16 = w_gate_up.astype(BF16)

    mod = _ada(c_all, w_ada, b_ada)
    shift, scale, gate = mod[:, :d], mod[:, d:2 * d], mod[:, 2 * d:]
    rep = lambda a: jnp.repeat(a[:n_seq], n_tok, axis=0)
    q_scale = 1.0 / math.sqrt(SB_HEAD_DIM)

    def in_proj(h, tm, want_k16):
        ua = _proj(h, w_a, F32, tm, 1024, name="proj_gla")
        glr = _proj(h, w_glr, F32, tm, GLA_RANK, name="proj_glr")
        (q16,) = _proj_headnorm(h, w_qb, q_norm_g, tm, 1024, (BF16,), q_scale, "proj_qb")
        k_out = _proj_headnorm(h, w_kb, k_norm_g, tm, 1024,
                               (F32, BF16) if want_k16 else (F32,), 1.0, "proj_kb")
        v_out = _proj_dual(h, w_vb, tm, 1024, "proj_vb")
        uc = _proj(h, w_c, F32, tm, 1024, name="proj_gates")
        return ua, glr, q16, k_out, v_out, uc

    hp = _mod_norm(xp, norm_g, scale[n_seq:n_seq + 1], shift[n_seq:n_seq + 1], 512)
    ua, glr, q16, (k32, k16), (v32, v16), uc = in_proj(hp, 1024, True)
    oa, s_p = _gla_prompt(ua, glr, wg16, b_gate, gla_norm_g, dk, dv)
    ob = _sb_prompt(q16, k16, v16, uc, sb_bias)
    y = _merge(oa, ob, wa16, wb16, uc, sb_w, sb_w + d, 512, 1024)
    out_p = _out_proj(y, wo16, xp, gate[n_seq:n_seq + 1], 1024, 1024)
    kp, vp = k32, v32

    m_s = xs.shape[0]
    hs = _mod_norm(xs, norm_g, rep(scale), rep(shift), m_s)
    ua, glr, q16, (k32,), (v32, _), uc = in_proj(hs, m_s, False)
    ua3 = _pad_rows(ua.reshape(n_seq, n_tok, -1), GLA_PAD_ROWS)
    glr3 = _pad_rows(glr.reshape(n_seq, n_tok, -1), GLA_PAD_ROWS)
    oa3, s_s = _gla_sample(ua3, glr3, wg16, b_gate, gla_norm_g, state, n_tok)
    oa = oa3[:, :n_tok].reshape(m_s, val_w).astype(BF16)

    ngrp = heads // SB_GROUP
    qg = q16.reshape(n_seq, n_tok, ngrp, SB_GROUP, SB_HEAD_DIM).transpose(0, 2, 3, 1, 4)
    eye = jnp.eye(SB_GROUP, dtype=BF16)[None, None, :, None, :, None]
    qbd = (qg[:, :, :, :, None, :] * eye).reshape(n_seq, ngrp, SB_GROUP * n_tok, SB_GW)
    z_g = (uc[:, :sb_w].reshape(n_seq, n_tok, ngrp, SB_GROUP, SB_HEAD_DIM)
           .transpose(0, 2, 3, 1, 4).reshape(n_seq, ngrp, SB_GROUP * n_tok, SB_HEAD_DIM))
    bias_rows = jnp.broadcast_to(jnp.repeat(sb_bias.astype(F32), n_tok)[:, None],
                                 (heads * n_tok, SB_HEAD_DIM))
    k_new = _pad_rows(k32.reshape(n_seq, n_tok, sb_w), GLA_PAD_ROWS)
    v_new = _pad_rows(v32.reshape(n_seq, n_tok, sb_w), GLA_PAD_ROWS)
    ob_g = _sb_sample(page_table, qbd, cache_k, cache_v, k_new, v_new, z_g, bias_rows, n_tok)
    ob = (ob_g.reshape(n_seq, ngrp, SB_GROUP, n_tok, SB_HEAD_DIM).transpose(0, 3, 1, 2, 4)
          .reshape(m_s, sb_w).astype(BF16))
    y = _merge(oa, ob, wa16, wb16, uc, sb_w, sb_w + d, m_s, 1024)
    out_s = _out_proj(y, wo16, xs, rep(gate), m_s, 1024)
    return out_p, out_s, kp, vp, s_p, k32, v32, s_s


def kernel(x_prompt, x_sample, cache_k, cache_v, state_gla, page_table, c_prompt, c_sample,
           w_ada, b_ada, norm_g, w_in, w_gate_up, b_gate, gla_norm_g, q_norm_g, k_norm_g,
           sb_bias, w_branch_a, w_branch_b, w_out):
    depth = w_ada.shape[0]
    n_b, t_len, d = x_prompt.shape
    assert n_b == 1, "the prompt group is a single sequence"
    n_seq, n_tok, _ = x_sample.shape
    heads = d // SB_HEAD_DIM
    n_phys, page = cache_k.shape[1:3]
    pad = (-(n_seq + n_b)) % 8
    c_all = jnp.concatenate([c_sample, c_prompt, jnp.zeros((pad, d), F32)], axis=0)
    xp = x_prompt.reshape(t_len, d)
    xs = x_sample.reshape(n_seq * n_tok, d)
    outs = [[] for _ in range(6)]
    for l in range(depth):
        w = (w_ada[l], b_ada[l], norm_g[l], w_in[l], w_gate_up[l], b_gate[l], gla_norm_g[l],
             q_norm_g[l], k_norm_g[l], sb_bias[l], w_branch_a[l], w_branch_b[l], w_out[l])
        ck = cache_k[l].reshape(n_phys, page, heads * SB_HEAD_DIM)
        cv = cache_v[l].reshape(n_phys, page, heads * SB_HEAD_DIM)
        xp, xs, kp, vp, sp, ks, vs, ss = _layer(xp, xs, c_all, w, ck, cv, state_gla[l],
                                                 page_table, n_seq, n_tok)
        outs[0].append(kp.reshape(n_b, t_len, heads, SB_HEAD_DIM))
        outs[1].append(vp.reshape(n_b, t_len, heads, SB_HEAD_DIM))
        outs[2].append(sp[None])
        outs[3].append(ks.reshape(n_seq, n_tok, heads, SB_HEAD_DIM))
        outs[4].append(vs.reshape(n_seq, n_tok, heads, SB_HEAD_DIM))
        outs[5].append(ss)
    k_p, v_p, s_p, k_s, v_s, s_s = (jnp.stack(o) for o in outs)
    return (xp.reshape(n_b, t_len, d), xs.reshape(n_seq, n_tok, d), k_p, v_p, s_p, k_s, v_s, s_s)
```

```python
import functools
import math

import numpy as np
import jax
import jax.numpy as jnp
from jax import lax
from jax.experimental import pallas as pl
from jax.experimental.pallas import tpu as pltpu

F32 = jnp.float32
BF16 = jnp.bfloat16

GLA_HEADS = 4
GLA_RANK = 16
GLA_TAU = 16.0
SB_HEAD_DIM = 128
NORM_EPS = 1e-6
LOG2E = 1.4426950408889634
LN2 = 0.6931471805599453

LANES = 128
VMEM_LIMIT = 56 * 1024 * 1024


def _cparams(sem):
    return pltpu.CompilerParams(dimension_semantics=sem, vmem_limit_bytes=VMEM_LIMIT)


def _sigmoid(x):
    return 1.0 / (1.0 + jnp.exp(-x))


def _softplus_parts(z):
    l = jnp.log(1.0 + jnp.exp(-jnp.abs(z)))
    return jnp.maximum(z, 0.0) + l, l


def _ada_kernel(c_ref, w_ref, b_ref, o_ref):
    c = c_ref[...]
    s = (c * _sigmoid(c)).astype(BF16)
    o_ref[...] = jnp.dot(s, w_ref[...].astype(BF16), preferred_element_type=F32) + b_ref[...]


def _ada(c_all, w_ada, b_ada, tn=512):
    r, d = c_all.shape
    n = w_ada.shape[1]
    return pl.pallas_call(
        _ada_kernel,
        out_shape=jax.ShapeDtypeStruct((r, n), F32),
        grid=(n // tn,),
        in_specs=[pl.BlockSpec((r, d), lambda j: (0, 0)),
                  pl.BlockSpec((d, tn), lambda j: (0, j)),
                  pl.BlockSpec((1, tn), lambda j: (0, j))],
        out_specs=pl.BlockSpec((r, tn), lambda j: (0, j)),
        compiler_params=_cparams(("arbitrary",)),
        name="ada_mod",
    )(c_all, w_ada, b_ada.reshape(1, n))


def _norm_kernel(x_ref, g_ref, scale_ref, shift_ref, h_ref):
    x = x_ref[...]
    ms = jnp.mean(x * x, axis=-1, keepdims=True)
    y = x * lax.rsqrt(ms + NORM_EPS) * g_ref[...]
    h_ref[...] = (y * (1.0 + scale_ref[...]) + shift_ref[...]).astype(h_ref.dtype)


def _mod_norm(x, g, scale, shift, tm):
    m, d = x.shape
    per_row = scale.shape[0] != 1
    mod_spec = (pl.BlockSpec((tm, d), lambda i: (i, 0)) if per_row
                else pl.BlockSpec((1, d), lambda i: (0, 0)))
    return pl.pallas_call(
        _norm_kernel,
        out_shape=jax.ShapeDtypeStruct((m, d), BF16),
        grid=(m // tm,),
        in_specs=[pl.BlockSpec((tm, d), lambda i: (i, 0)),
                  pl.BlockSpec((1, d), lambda i: (0, 0)),
                  mod_spec, mod_spec],
        out_specs=pl.BlockSpec((tm, d), lambda i: (i, 0)),
        compiler_params=_cparams(("arbitrary",)),
        name="mod_norm",
    )(x, g.reshape(1, d), scale, shift)


def _proj_kernel(h_ref, w_ref, o_ref, *, scale):
    acc = jnp.dot(h_ref[...], w_ref[...], preferred_element_type=F32)
    if scale != 1.0:
        acc = acc * scale
    o_ref[...] = acc.astype(o_ref.dtype)


def _proj(h, w, out_dtype, tm, tn, scale=1.0, name="proj"):
    m, k = h.shape
    n = w.shape[1]
    tn = min(tn, n)
    return pl.pallas_call(
        functools.partial(_proj_kernel, scale=scale),
        out_shape=jax.ShapeDtypeStruct((m, n), out_dtype),
        grid=(m // tm, n // tn),
        in_specs=[pl.BlockSpec((tm, k), lambda i, j: (i, 0)),
                  pl.BlockSpec((k, tn), lambda i, j: (0, j))],
        out_specs=pl.BlockSpec((tm, tn), lambda i, j: (i, j)),
        compiler_params=_cparams(("arbitrary", "arbitrary")),
        name=name,
    )(h, w)


def _proj_headnorm_kernel(h_ref, w_ref, g_ref, *o_refs, scale):
    acc = jnp.dot(h_ref[...], w_ref[...], preferred_element_type=F32)
    g = g_ref[...]
    tn = acc.shape[1]
    for c in range(tn // SB_HEAD_DIM):
        cols = slice(c * SB_HEAD_DIM, (c + 1) * SB_HEAD_DIM)
        blk = acc[:, cols]
        ms = jnp.mean(blk * blk, axis=-1, keepdims=True)
        y = blk * lax.rsqrt(ms + NORM_EPS) * g
        for o_ref in o_refs:
            if o_ref.dtype == F32:
                o_ref[:, cols] = y
            else:
                o_ref[:, cols] = (y * scale).astype(o_ref.dtype)


def _proj_headnorm(h, w, g, tm, tn, out_dtypes, scale, name):
    m, k = h.shape
    n = w.shape[1]
    tile = pl.BlockSpec((tm, tn), lambda i, j: (i, j))
    return pl.pallas_call(
        functools.partial(_proj_headnorm_kernel, scale=scale),
        out_shape=tuple(jax.ShapeDtypeStruct((m, n), dt) for dt in out_dtypes),
        grid=(m // tm, n // tn),
        in_specs=[pl.BlockSpec((tm, k), lambda i, j: (i, 0)),
                  pl.BlockSpec((k, tn), lambda i, j: (0, j)),
                  pl.BlockSpec((1, SB_HEAD_DIM), lambda i, j: (0, 0))],
        out_specs=tuple(tile for _ in out_dtypes),
        compiler_params=_cparams(("arbitrary", "arbitrary")),
        name=name,
    )(h, w, g.reshape(1, SB_HEAD_DIM))


def _proj_dual_kernel(h_ref, w_ref, o32_ref, o16_ref):
    acc = jnp.dot(h_ref[...], w_ref[...], preferred_element_type=F32)
    o32_ref[...] = acc
    o16_ref[...] = acc.astype(BF16)


def _proj_dual(h, w, tm, tn, name):
    m, k = h.shape
    n = w.shape[1]
    return pl.pallas_call(
        _proj_dual_kernel,
        out_shape=(jax.ShapeDtypeStruct((m, n), F32), jax.ShapeDtypeStruct((m, n), BF16)),
        grid=(m // tm, n // tn),
        in_specs=[pl.BlockSpec((tm, k), lambda i, j: (i, 0)),
                  pl.BlockSpec((k, tn), lambda i, j: (0, j))],
        out_specs=(pl.BlockSpec((tm, tn), lambda i, j: (i, j)),
                   pl.BlockSpec((tm, tn), lambda i, j: (i, j))),
        compiler_params=_cparams(("arbitrary", "arbitrary")),
        name=name,
    )(h, w)


def _merge_kernel(oa_ref, ob_ref, wa_ref, wb_ref, ga_ref, gb_ref, y_ref):
    ya = jnp.dot(oa_ref[...], wa_ref[...], preferred_element_type=F32)
    yb = jnp.dot(ob_ref[...], wb_ref[...], preferred_element_type=F32)
    y_ref[...] = (_sigmoid(ga_ref[...]) * ya + _sigmoid(gb_ref[...]) * yb).astype(y_ref.dtype)


def _merge(oa, ob, wa, wb, uc, ga_col, gb_col, tm, tn):
    m, k = oa.shape
    n = wa.shape[1]
    row = lambda i, j: (i, 0)
    col = lambda i, j: (0, j)
    ja, jb = ga_col // tn, gb_col // tn
    return pl.pallas_call(
        _merge_kernel,
        out_shape=jax.ShapeDtypeStruct((m, n), BF16),
        grid=(m // tm, n // tn),
        in_specs=[pl.BlockSpec((tm, k), row), pl.BlockSpec((tm, k), row),
                  pl.BlockSpec((k, tn), col), pl.BlockSpec((k, tn), col),
                  pl.BlockSpec((tm, tn), lambda i, j: (i, ja + j)),
                  pl.BlockSpec((tm, tn), lambda i, j: (i, jb + j))],
        out_specs=pl.BlockSpec((tm, tn), lambda i, j: (i, j)),
        compiler_params=_cparams(("arbitrary", "arbitrary")),
        name="merge",
    )(oa, ob, wa, wb, uc, uc)


def _out_kernel(y_ref, w_ref, x_ref, gate_ref, o_ref):
    acc = jnp.dot(y_ref[...], w_ref[...], preferred_element_type=F32)
    o_ref[...] = x_ref[...] + gate_ref[...] * acc


def _out_proj(y, w, x, gate, tm, tn):
    m, k = y.shape
    n = w.shape[1]
    per_row = gate.shape[0] != 1
    gate_spec = (pl.BlockSpec((tm, tn), lambda i, j: (i, j)) if per_row
                 else pl.BlockSpec((1, tn), lambda i, j: (0, j)))
    return pl.pallas_call(
        _out_kernel,
        out_shape=jax.ShapeDtypeStruct((m, n), F32),
        grid=(m // tm, n // tn),
        in_specs=[pl.BlockSpec((tm, k), lambda i, j: (i, 0)),
                  pl.BlockSpec((k, tn), lambda i, j: (0, j)),
                  pl.BlockSpec((tm, tn), lambda i, j: (i, j)),
                  gate_spec],
        out_specs=pl.BlockSpec((tm, tn), lambda i, j: (i, j)),
        compiler_params=_cparams(("arbitrary", "arbitrary")),
        name="out_proj",
    )(y, w, x, gate)


GLA_SUB = 16
GLA_ROWS = 256


def _split3(x):
    hi = x.astype(BF16)
    r1 = x - hi.astype(F32)
    mid = r1.astype(BF16)
    lo = (r1 - mid.astype(F32)).astype(BF16)
    return hi, mid, lo


def _gla_prompt_kernel(q_ref, k_ref, v_ref, z_ref, glr_ref, wg_ref, bg_ref, gn_ref, tri_ref,
                       o_ref, s_ref, st_ref, b_ref):
    t = pl.program_id(1)
    nt = pl.num_programs(1)
    rows, dk = q_ref.shape
    dv = v_ref.shape[1]
    c = GLA_SUB

    @pl.when(t == 0)
    def _():
        st_ref[...] = jnp.zeros_like(st_ref)

    x = jnp.dot(glr_ref[...].astype(BF16), wg_ref[...], preferred_element_type=F32) + bg_ref[...]
    la = (jnp.minimum(x, 0.0) - jnp.log(1.0 + jnp.exp(-jnp.abs(x)))) * (1.0 / GLA_TAU)
    tri = tri_ref[...]
    hi, mid, lo = _split3(la)
    b_ref[...] = (jnp.dot(tri, hi, preferred_element_type=F32)
                  + jnp.dot(tri, mid, preferred_element_type=F32)
                  + jnp.dot(tri, lo, preferred_element_type=F32))

    row_id = lax.broadcasted_iota(jnp.int32, (c, 1), 0)
    gn = gn_ref[...]

    def chunk(ci, carry):
        r0 = pl.multiple_of(ci * c, c)
        qc = q_ref[pl.ds(r0, c), :]
        kc = k_ref[pl.ds(r0, c), :]
        vc = v_ref[pl.ds(r0, c), :]
        bc = b_ref[pl.ds(r0, c), :]
        b_end = b_ref[pl.ds(r0 + c - 1, 1), :]
        st = st_ref[...]
        qd = (qc * jnp.exp(bc)).astype(BF16)
        o = lax.dot_general(qd, st.astype(BF16), (((1,), (1,)), ((), ())),
                            preferred_element_type=F32)
        for s in range(c):
            ks = k_ref[pl.ds(r0 + s, 1), :]
            bs = b_ref[pl.ds(r0 + s, 1), :]
            vs = v_ref[pl.ds(r0 + s, 1), :]
            w = jnp.where(row_id >= s, qc * ks * jnp.exp(bc - bs), 0.0)
            o = o + jnp.sum(w, axis=-1, keepdims=True) * vs
        ms = jnp.mean(o * o, axis=-1, keepdims=True)
        zc = z_ref[pl.ds(r0, c), :]
        y = o * lax.rsqrt(ms + NORM_EPS) * gn * (zc * _sigmoid(zc))
        o_ref[pl.ds(r0, c), :] = y.astype(o_ref.dtype)
        kd = (kc * jnp.exp(b_end - bc)).astype(BF16)
        upd = lax.dot_general(vc.astype(BF16), kd, (((0,), (0,)), ((), ())),
                              preferred_element_type=F32)
        st_ref[...] = st * jnp.exp(b_end) + upd
        return carry

    lax.fori_loop(0, rows // c, chunk, 0)

    @pl.when(t == nt - 1)
    def _():
        s_ref[0] = st_ref[...].T


def _gla_prompt(ua, glr, wg, bg, gn, dk, dv):
    t_len = ua.shape[0]
    h = GLA_HEADS
    rows = min(GLA_ROWS, t_len)
    r = np.arange(rows)
    tri = ((r[:, None] >= r[None, :]) & (r[:, None] // GLA_SUB == r[None, :] // GLA_SUB))
    tri = jnp.asarray(tri, dtype=BF16)
    k_blk = h
    v_blk = 2 * h * dk // dv
    z_blk = v_blk + h
    return pl.pallas_call(
        _gla_prompt_kernel,
        out_shape=(jax.ShapeDtypeStruct((t_len, h * dv), BF16),
                   jax.ShapeDtypeStruct((h, dk, dv), F32)),
        grid=(h, t_len // rows),
        in_specs=[pl.BlockSpec((rows, dk), lambda hh, tt: (tt, hh)),
                  pl.BlockSpec((rows, dk), lambda hh, tt: (tt, k_blk + hh)),
                  pl.BlockSpec((rows, dv), lambda hh, tt: (tt, v_blk + hh)),
                  pl.BlockSpec((rows, dv), lambda hh, tt: (tt, z_blk + hh)),
                  pl.BlockSpec((rows, GLA_RANK), lambda hh, tt: (tt, 0)),
                  pl.BlockSpec((GLA_RANK, dk), lambda hh, tt: (0, hh)),
                  pl.BlockSpec((1, dk), lambda hh, tt: (0, hh)),
                  pl.BlockSpec((1, dv), lambda hh, tt: (0, 0)),
                  pl.BlockSpec((rows, rows), lambda hh, tt: (0, 0))],
        out_specs=(pl.BlockSpec((rows, dv), lambda hh, tt: (tt, hh)),
                   pl.BlockSpec((1, dk, dv), lambda hh, tt: (hh, 0, 0))),
        scratch_shapes=[pltpu.VMEM((dv, dk), F32), pltpu.VMEM((rows, dk), F32)],
        compiler_params=_cparams(("arbitrary", "arbitrary")),
        name="gla_prompt",
    )(ua, ua, ua, ua, glr, wg, bg.reshape(1, -1), gn.reshape(1, -1), tri)


GLA_PAD_ROWS = 8


def _gla_sample_kernel(q_ref, k_ref, v_ref, z_ref, glr_ref, wg_ref, bg_ref, gn_ref, s_ref,
                       o_ref, so_ref, *, n_tok):
    rows = q_ref.shape[1]
    dk = s_ref.shape[2]
    dv = s_ref.shape[3]
    row_id = lax.broadcasted_iota(jnp.int32, (rows, 1), 0)
    gn = gn_ref[...]
    glr = glr_ref[0].astype(BF16)
    for h in range(GLA_HEADS):
        ksl = slice(h * dk, (h + 1) * dk)
        vsl = slice(h * dv, (h + 1) * dv)
        x = jnp.dot(glr, wg_ref[:, ksl], preferred_element_type=F32) + bg_ref[:, ksl]
        la = (jnp.minimum(x, 0.0) - jnp.log(1.0 + jnp.exp(-jnp.abs(x)))) * (1.0 / GLA_TAU)
        bc = jnp.zeros_like(la)
        for s in range(n_tok):
            bc = bc + jnp.where(row_id >= s, la[s:s + 1, :], 0.0)
        b_end = bc[n_tok - 1:n_tok, :]
        qc = q_ref[0, :, ksl]
        kc = k_ref[0, :, ksl]
        vc = v_ref[0, :, vsl]
        st = s_ref[0, h].T
        qd = (qc * jnp.exp(bc)).astype(BF16)
        o = lax.dot_general(qd, st.astype(BF16), (((1,), (1,)), ((), ())),
                            preferred_element_type=F32)
        for s in range(n_tok):
            w = jnp.where(row_id >= s, qc * kc[s:s + 1, :] * jnp.exp(bc - bc[s:s + 1, :]), 0.0)
            o = o + jnp.sum(w, axis=-1, keepdims=True) * vc[s:s + 1, :]
        ms = jnp.mean(o * o, axis=-1, keepdims=True)
        zc = z_ref[0, :, vsl]
        o_ref[0, :, vsl] = o * lax.rsqrt(ms + NORM_EPS) * gn * (zc * _sigmoid(zc))
        kd = (kc * jnp.exp(b_end - bc)).astype(BF16)
        upd = lax.dot_general(vc.astype(BF16), kd, (((0,), (0,)), ((), ())),
                              preferred_element_type=F32)
        so_ref[0, h] = (st * jnp.exp(b_end) + upd).T


def _gla_sample(ua, glr, wg, bg, gn, state, n_tok):
    b, rows, _ = ua.shape
    h, dk, dv = state.shape[1:]
    k_blk = h * dk // (h * dk)
    v_blk = 2 * h * dk // (h * dv)
    z_blk = v_blk + 1
    return pl.pallas_call(
        functools.partial(_gla_sample_kernel, n_tok=n_tok),
        out_shape=(jax.ShapeDtypeStruct((b, rows, h * dv), F32),
                   jax.ShapeDtypeStruct(state.shape, F32)),
        grid=(b,),
        in_specs=[pl.BlockSpec((1, rows, h * dk), lambda i: (i, 0, 0)),
                  pl.BlockSpec((1, rows, h * dk), lambda i: (i, 0, k_blk)),
                  pl.BlockSpec((1, rows, h * dv), lambda i: (i, 0, v_blk)),
                  pl.BlockSpec((1, rows, h * dv), lambda i: (i, 0, z_blk)),
                  pl.BlockSpec((1, rows, GLA_RANK), lambda i: (i, 0, 0)),
                  pl.BlockSpec((GLA_RANK, h * dk), lambda i: (0, 0)),
                  pl.BlockSpec((1, h * dk), lambda i: (0, 0)),
                  pl.BlockSpec((1, dv), lambda i: (0, 0)),
                  pl.BlockSpec((1, h, dk, dv), lambda i: (i, 0, 0, 0))],
        out_specs=(pl.BlockSpec((1, rows, h * dv), lambda i: (i, 0, 0)),
                   pl.BlockSpec((1, h, dk, dv), lambda i: (i, 0, 0, 0))),
        compiler_params=_cparams(("arbitrary",)),
        name="gla_sample",
    )(ua, ua, ua, ua, glr, wg, bg.reshape(1, -1), gn.reshape(1, -1), state)


SB_BLK = 1024
SB_SUB = 128


def _suffix_matrix():
    j = np.arange(SB_SUB)
    u = (j[:, None] >= j[None, :]).astype(np.float32)
    half = np.concatenate([u, np.ones((SB_SUB, SB_SUB), np.float32)], axis=1)
    return jnp.asarray(np.concatenate([half, half], axis=0), dtype=BF16)


def _sb_suffix(z, uw, mask):
    neg_abs = lax.bitcast_convert_type(
        lax.bitcast_convert_type(z, jnp.int32) | jnp.int32(-2 ** 31), F32)
    sp = jnp.maximum(z, 0.0) + jnp.log(1.0 + jnp.exp2(neg_abs)) * LOG2E
    if mask is not None:
        sp = jnp.where(mask, sp, 0.0)
    hi = sp.astype(BF16)
    lo = (sp - hi.astype(F32)).astype(BF16)
    return jnp.dot(jnp.concatenate([hi, lo], axis=1), uw, preferred_element_type=F32)


def _sb_weights(z, s2, car, mask):
    a = jnp.exp2(z - s2[:, :SB_SUB] - car)
    if mask is not None:
        a = jnp.where(mask, a, 0.0)
    return a, car + s2[:, SB_SUB:]


def _sb_prompt_kernel(qi_ref, kj_ref, q_ref, k_ref, v_ref, z_ref, bias_ref, uw_ref,
                      o_ref, acc_ref, car_ref):
    step = pl.program_id(1)
    qi = qi_ref[step]
    kj = kj_ref[step]
    blk = q_ref.shape[0]
    nsub = blk // SB_SUB
    uw = uw_ref[...]
    bias = bias_ref[0]

    @pl.when(kj == qi)
    def _():
        acc_ref[...] = jnp.zeros_like(acc_ref)
        car_ref[...] = jnp.zeros_like(car_ref)

    def logits(ks, r0):
        return lax.dot_general(q_ref[r0:blk, :], k_ref[ks * SB_SUB:(ks + 1) * SB_SUB, :],
                               (((1,), (1,)), ((), ())), preferred_element_type=F32) + bias

    def causal_mask(shape, masked):
        if not masked:
            return None
        return lax.broadcasted_iota(jnp.int32, shape, 1) < lax.broadcasted_iota(jnp.int32, shape, 0)

    def finish(ks, r0, z, s2, mask):
        a, car = _sb_weights(z, s2, car_ref[r0:blk, :], mask)
        car_ref[r0:blk, :] = car
        acc_ref[r0:blk, :] += jnp.dot(a.astype(BF16), v_ref[ks * SB_SUB:(ks + 1) * SB_SUB, :],
                                      preferred_element_type=F32)

    def sweep(masked):
        tiles = [(ks, ks * SB_SUB if masked else 0) for ks in range(nsub - 1, -1, -1)]
        z = logits(*tiles[0])
        mask = causal_mask(z.shape, masked)
        s2 = _sb_suffix(z, uw, mask)
        for i, (ks, r0) in enumerate(tiles):
            nxt = tiles[i + 1] if i + 1 < len(tiles) else None
            if nxt is not None:
                z_n = logits(*nxt)
                mask_n = causal_mask(z_n.shape, masked)
            finish(ks, r0, z, s2, mask)
            if nxt is not None:
                s2 = _sb_suffix(z_n, uw, mask_n)
                z, mask = z_n, mask_n

    @pl.when(kj == qi)
    def _():
        sweep(True)

    @pl.when(kj != qi)
    def _():
        sweep(False)

    @pl.when(kj == 0)
    def _():
        zz = z_ref[...]
        o_ref[...] = (acc_ref[...] * (zz * _sigmoid(zz))).astype(o_ref.dtype)


def _sb_prompt(q16, k16, v16, uc, bias):
    t_len, w = q16.shape
    heads = w // SB_HEAD_DIM
    blk = min(SB_BLK, t_len)
    nq = t_len // blk
    qi = np.concatenate([np.full(i + 1, i) for i in range(nq)]).astype(np.int32)
    kj = np.concatenate([np.arange(i, -1, -1) for i in range(nq)]).astype(np.int32)
    bias_b = jnp.broadcast_to((bias.astype(F32) * LOG2E)[:, None, None], (heads, 1, SB_HEAD_DIM))
    qmap = lambda h, s, qi_r, kj_r: (qi_r[s], h)
    kmap = lambda h, s, qi_r, kj_r: (kj_r[s], h)
    return pl.pallas_call(
        _sb_prompt_kernel,
        out_shape=jax.ShapeDtypeStruct((t_len, w), BF16),
        grid_spec=pltpu.PrefetchScalarGridSpec(
            num_scalar_prefetch=2,
            grid=(heads, len(qi)),
            in_specs=[pl.BlockSpec((blk, SB_HEAD_DIM), qmap),
                      pl.BlockSpec((blk, SB_HEAD_DIM), kmap),
                      pl.BlockSpec((blk, SB_HEAD_DIM), kmap),
                      pl.BlockSpec((blk, SB_HEAD_DIM), qmap),
                      pl.BlockSpec((1, 1, SB_HEAD_DIM), lambda h, s, qi_r, kj_r: (h, 0, 0)),
                      pl.BlockSpec((2 * SB_SUB, 2 * SB_SUB), lambda h, s, qi_r, kj_r: (0, 0))],
            out_specs=pl.BlockSpec((blk, SB_HEAD_DIM), qmap),
            scratch_shapes=[pltpu.VMEM((blk, SB_HEAD_DIM), F32),
                            pltpu.VMEM((blk, SB_HEAD_DIM), F32)]),
        compiler_params=_cparams(("arbitrary", "arbitrary")),
        name="sb_prompt",
    )(jnp.asarray(qi), jnp.asarray(kj), q16, k16, v16, uc, bias_b, _suffix_matrix())


SB_PAIR = 2


def _sb_sample_kernel(pt_ref, q_ref, kc_ref, vc_ref, kn_ref, vn_ref, z_ref, bias_ref, uw_ref,
                      o_ref, acc_ref, car_ref, kpg_ref, vpg_ref, *, n_tok):
    b = pl.program_id(0)
    p = pl.program_id(1)
    n_pages = pl.num_programs(1)
    heads = q_ref.shape[1]
    rows = q_ref.shape[2]
    page_len = kc_ref.shape[1] // heads
    uw = uw_ref[...]
    bias = bias_ref[...]

    def head_rows(page_ref, h):
        return page_ref[pl.ds(h, page_len, stride=heads), :].astype(BF16)

    def page(k_page_ref, v_page_ref, masked):
        zs = []
        for g in range(heads // SB_PAIR):
            zg = None
            for h in range(g * SB_PAIR, (g + 1) * SB_PAIR):
                zh = lax.dot_general(q_ref[0, h], head_rows(k_page_ref, h), (((1,), (1,)), ((), ())),
                                     preferred_element_type=F32)
                zg = zh if zg is None else zg + zh
            zs.append(zg)
        z = jnp.concatenate(zs, axis=0) + bias
        mask = None
        if masked:
            tok = lax.broadcasted_iota(jnp.int32, z.shape, 0) % n_tok
            mask = lax.broadcasted_iota(jnp.int32, z.shape, 1) < tok
        a, car = _sb_weights(z, _sb_suffix(z, uw, mask), car_ref[...], mask)
        car_ref[...] = car
        a16 = a.astype(BF16)
        for h in range(heads):
            g = h // SB_PAIR
            acc_ref[h] += jnp.dot(a16[g * rows:(g + 1) * rows, :], head_rows(v_page_ref, h),
                                  preferred_element_type=F32)

    @pl.when(jnp.logical_and(b == 0, p == 0))
    def _():
        kpg_ref[...] = jnp.zeros_like(kpg_ref)
        vpg_ref[...] = jnp.zeros_like(vpg_ref)

    @pl.when(p == 0)
    def _():
        acc_ref[...] = jnp.zeros_like(acc_ref)
        car_ref[...] = jnp.zeros_like(car_ref)
        kpg_ref[0:kn_ref.shape[1], :] = kn_ref[0]
        vpg_ref[0:vn_ref.shape[1], :] = vn_ref[0]
        page(kpg_ref, vpg_ref, True)

    page(kc_ref.at[0], vc_ref.at[0], False)

    @pl.when(p == n_pages - 1)
    def _():
        first = lax.broadcasted_iota(jnp.int32, (rows, SB_HEAD_DIM), 0) < n_tok
        for g in range(heads // SB_PAIR):
            og = jnp.where(first, acc_ref[g * SB_PAIR], acc_ref[g * SB_PAIR + 1])
            zz = z_ref[0, g]
            o_ref[0, g] = og * (zz * _sigmoid(zz))


def _sb_sample(page_table, q8, cache_k, cache_v, k_new, v_new, z_g, bias_rows, n_tok):
    b, heads, rows, _ = q8.shape
    ngrp = heads // SB_PAIR
    n_pages = page_table.shape[1]
    prow = cache_k.shape[1]
    cmap = lambda i, p, pt: (pt[i, n_pages - 1 - p], 0, 0)
    smap3 = lambda i, p, pt: (i, 0, 0)
    smap4 = lambda i, p, pt: (i, 0, 0, 0)
    return pl.pallas_call(
        functools.partial(_sb_sample_kernel, n_tok=n_tok),
        out_shape=jax.ShapeDtypeStruct((b, ngrp, rows, SB_HEAD_DIM), F32),
        grid_spec=pltpu.PrefetchScalarGridSpec(
            num_scalar_prefetch=1,
            grid=(b, n_pages),
            in_specs=[pl.BlockSpec((1, heads, rows, SB_HEAD_DIM), smap4),
                      pl.BlockSpec((1, prow, SB_HEAD_DIM), cmap),
                      pl.BlockSpec((1, prow, SB_HEAD_DIM), cmap),
                      pl.BlockSpec((1, k_new.shape[1], SB_HEAD_DIM), smap3),
                      pl.BlockSpec((1, v_new.shape[1], SB_HEAD_DIM), smap3),
                      pl.BlockSpec((1, ngrp, rows, SB_HEAD_DIM), smap4),
                      pl.BlockSpec((ngrp * rows, SB_HEAD_DIM), lambda i, p, pt: (0, 0)),
                      pl.BlockSpec((2 * SB_SUB, 2 * SB_SUB), lambda i, p, pt: (0, 0))],
            out_specs=pl.BlockSpec((1, ngrp, rows, SB_HEAD_DIM), smap4),
            scratch_shapes=[pltpu.VMEM((heads, rows, SB_HEAD_DIM), F32),
                            pltpu.VMEM((ngrp * rows, SB_HEAD_DIM), F32),
                            pltpu.VMEM((prow, SB_HEAD_DIM), F32),
                            pltpu.VMEM((prow, SB_HEAD_DIM), F32)]),
        compiler_params=_cparams(("arbitrary", "arbitrary")),
        name="sb_sample",
    )(page_table, q8, cache_k, cache_v, k_new, v_new, z_g, bias_rows, _suffix_matrix())


def _pad_rows(a, rows):
    return jnp.pad(a, ((0, 0), (0, rows - a.shape[1]), (0, 0)))


def _layer(xp, xs, c_all, w, cache_k, cache_v, state, page_table, n_seq, n_tok):
    (w_ada, b_ada, norm_g, w_in, w_gate_up, b_gate, gla_norm_g,
     q_norm_g, k_norm_g, sb_bias, w_branch_a, w_branch_b, w_out) = w
    d = xp.shape[1]
    heads = d // SB_HEAD_DIM
    dv = gla_norm_g.shape[0]
    key_w = w_gate_up.shape[1]
    dk = key_w // GLA_HEADS
    val_w = GLA_HEADS * dv
    sb_w = heads * SB_HEAD_DIM

    c0 = 2 * key_w + 2 * val_w
    c1 = c0 + GLA_RANK
    w_a = jnp.concatenate([w_in[:, :key_w] * (dk ** -0.5), w_in[:, key_w:c0]], axis=1).astype(BF16)
    w_glr = w_in[:, c0:c1].astype(BF16)
    w_qb = w_in[:, c1:c1 + sb_w].astype(BF16)
    w_kb = w_in[:, c1 + sb_w:c1 + 2 * sb_w].astype(BF16)
    w_vb = w_in[:, c1 + 2 * sb_w:c1 + 3 * sb_w].astype(BF16)
    w_c = w_in[:, c1 + 3 * sb_w:].astype(BF16)
    wa16, wb16, wo16 = (w_branch_a.astype(BF16), w_branch_b.astype(BF16), w_out.astype(BF16))
    wg16 = w_gate_up.astype(BF16)

    mod = _ada(c_all, w_ada, b_ada)
    shift, scale, gate = mod[:, :d], mod[:, d:2 * d], mod[:, 2 * d:]
    rep = lambda a: jnp.repeat(a[:n_seq], n_tok, axis=0)
    q_scale = LOG2E / math.sqrt(SB_HEAD_DIM)

    def in_proj(h, tm, want_k16):
        ua = _proj(h, w_a, F32, tm, 1024, name="proj_gla")
        glr = _proj(h, w_glr, F32, tm, GLA_RANK, name="proj_glr")
        (q16,) = _proj_headnorm(h, w_qb, q_norm_g, tm, 1024, (BF16,), q_scale, "proj_qb")
        k_out = _proj_headnorm(h, w_kb, k_norm_g, tm, 1024,
                               (F32, BF16) if want_k16 else (F32,), 1.0, "proj_kb")
        v_out = _proj_dual(h, w_vb, tm, 1024, "proj_vb")
        uc = _proj(h, w_c, F32, tm, 1024, name="proj_gates")
        return ua, glr, q16, k_out, v_out, uc

    hp = _mod_norm(xp, norm_g, scale[n_seq:n_seq + 1], shift[n_seq:n_seq + 1], 512)
    ua, glr, q16, (k32, k16), (v32, v16), uc = in_proj(hp, 1024, True)
    oa, s_p = _gla_prompt(ua, glr, wg16, b_gate, gla_norm_g, dk, dv)
    ob = _sb_prompt(q16, k16, v16, uc, sb_bias)
    y = _merge(oa, ob, wa16, wb16, uc, sb_w, sb_w + d, 512, 1024)
    out_p = _out_proj(y, wo16, xp, gate[n_seq:n_seq + 1], 1024, 1024)
    kp, vp = k32, v32

    m_s = xs.shape[0]
    hs = _mod_norm(xs, norm_g, rep(scale), rep(shift), m_s)
    ua, glr, q16, (k32,), (v32, _), uc = in_proj(hs, m_s, False)
    ua3 = _pad_rows(ua.reshape(n_seq, n_tok, -1), GLA_PAD_ROWS)
    glr3 = _pad_rows(glr.reshape(n_seq, n_tok, -1), GLA_PAD_ROWS)
    oa3, s_s = _gla_sample(ua3, glr3, wg16, b_gate, gla_norm_g, state, n_tok)
    oa = oa3[:, :n_tok].reshape(m_s, val_w).astype(BF16)

    ngrp = heads // SB_PAIR
    qg = q16.reshape(n_seq, n_tok, ngrp, SB_PAIR, SB_HEAD_DIM).transpose(0, 2, 3, 1, 4)
    eye = jnp.eye(SB_PAIR, dtype=BF16)[None, None, :, :, None, None]
    q8 = (qg[:, :, :, None, :, :] * eye).reshape(n_seq, heads, SB_PAIR * n_tok, SB_HEAD_DIM)
    z_g = (uc[:, :sb_w].reshape(n_seq, n_tok, ngrp, SB_PAIR, SB_HEAD_DIM)
           .transpose(0, 2, 3, 1, 4).reshape(n_seq, ngrp, SB_PAIR * n_tok, SB_HEAD_DIM))
    bias_rows = jnp.broadcast_to(jnp.repeat(sb_bias.astype(F32) * LOG2E, n_tok)[:, None],
                                 (heads * n_tok, SB_HEAD_DIM))
    k_new = k32.reshape(n_seq, n_tok * heads, SB_HEAD_DIM)
    v_new = v32.reshape(n_seq, n_tok * heads, SB_HEAD_DIM)
    ob_g = _sb_sample(page_table, q8, cache_k, cache_v, k_new, v_new, z_g, bias_rows, n_tok)
    ob = (ob_g.reshape(n_seq, ngrp, SB_PAIR, n_tok, SB_HEAD_DIM).transpose(0, 3, 1, 2, 4)
          .reshape(m_s, sb_w).astype(BF16))
    y = _merge(oa, ob, wa16, wb16, uc, sb_w, sb_w + d, m_s, 1024)
    out_s = _out_proj(y, wo16, xs, rep(gate), m_s, 1024)
    return out_p, out_s, kp, vp, s_p, k32, v32, s_s


def kernel(x_prompt, x_sample, cache_k, cache_v, state_gla, page_table, c_prompt, c_sample,
           w_ada, b_ada, norm_g, w_in, w_gate_up, b_gate, gla_norm_g, q_norm_g, k_norm_g,
           sb_bias, w_branch_a, w_branch_b, w_out):
    depth = w_ada.shape[0]
    n_b, t_len, d = x_prompt.shape
    assert n_b == 1, "the prompt group is a single sequence"
    n_seq, n_tok, _ = x_sample.shape
    heads = d // SB_HEAD_DIM
    n_phys, page = cache_k.shape[1:3]
    pad = (-(n_seq + n_b)) % 8
    c_all = jnp.concatenate([c_sample, c_prompt, jnp.zeros((pad, d), F32)], axis=0)
    xp = x_prompt.reshape(t_len, d)
    xs = x_sample.reshape(n_seq * n_tok, d)
    outs = [[] for _ in range(6)]
    for l in range(depth):
        w = (w_ada[l], b_ada[l], norm_g[l], w_in[l], w_gate_up[l], b_gate[l], gla_norm_g[l],
             q_norm_g[l], k_norm_g[l], sb_bias[l], w_branch_a[l], w_branch_b[l], w_out[l])
        ck = cache_k[l].reshape(n_phys, page * heads, SB_HEAD_DIM)
        cv = cache_v[l].reshape(n_phys, page * heads, SB_HEAD_DIM)
        xp, xs, kp, vp, sp, ks, vs, ss = _layer(xp, xs, c_all, w, ck, cv, state_gla[l],
                                                 page_table, n_seq, n_tok)
        outs[0].append(kp.reshape(n_b, t_len, heads, SB_HEAD_DIM))
        outs[1].append(vp.reshape(n_b, t_len, heads, SB_HEAD_DIM))
        outs[2].append(sp[None])
        outs[3].append(ks.reshape(n_seq, n_tok, heads, SB_HEAD_DIM))
        outs[4].append(vs.reshape(n_seq, n_tok, heads, SB_HEAD_DIM))
        outs[5].append(ss)
    k_p, v_p, s_p, k_s, v_s, s_s = (jnp.stack(o) for o in outs)
    return (xp.reshape(n_b, t_len, d), xs.reshape(n_seq, n_tok, d), k_p, v_p, s_p, k_s, v_s, s_s)
```

```python
import functools
import math

import numpy as np
import jax
import jax.numpy as jnp
from jax import lax
from jax.experimental import pallas as pl
from jax.experimental.pallas import tpu as pltpu

F32 = jnp.float32
BF16 = jnp.bfloat16

GLA_HEADS = 4
GLA_RANK = 16
GLA_TAU = 16.0
SB_HEAD_DIM = 128
NORM_EPS = 1e-6
LOG2E = 1.4426950408889634
LN2 = 0.6931471805599453

LANES = 128
VMEM_LIMIT = 56 * 1024 * 1024


def _cparams(sem):
    return pltpu.CompilerParams(dimension_semantics=sem, vmem_limit_bytes=VMEM_LIMIT)


def _sigmoid(x):
    return 1.0 / (1.0 + jnp.exp(-x))


def _softplus_parts(z):
    l = jnp.log(1.0 + jnp.exp(-jnp.abs(z)))
    return jnp.maximum(z, 0.0) + l, l


def _ada_kernel(c_ref, w_ref, b_ref, o_ref):
    c = c_ref[...]
    s = (c * _sigmoid(c)).astype(BF16)
    o_ref[...] = jnp.dot(s, w_ref[...].astype(BF16), preferred_element_type=F32) + b_ref[...]


def _ada(c_all, w_ada, b_ada, tn=512):
    r, d = c_all.shape
    n = w_ada.shape[1]
    return pl.pallas_call(
        _ada_kernel,
        out_shape=jax.ShapeDtypeStruct((r, n), F32),
        grid=(n // tn,),
        in_specs=[pl.BlockSpec((r, d), lambda j: (0, 0)),
                  pl.BlockSpec((d, tn), lambda j: (0, j)),
                  pl.BlockSpec((1, tn), lambda j: (0, j))],
        out_specs=pl.BlockSpec((r, tn), lambda j: (0, j)),
        compiler_params=_cparams(("arbitrary",)),
        name="ada_mod",
    )(c_all, w_ada, b_ada.reshape(1, n))


def _norm_kernel(x_ref, g_ref, scale_ref, shift_ref, h_ref):
    x = x_ref[...]
    ms = jnp.mean(x * x, axis=-1, keepdims=True)
    y = x * lax.rsqrt(ms + NORM_EPS) * g_ref[...]
    h_ref[...] = (y * (1.0 + scale_ref[...]) + shift_ref[...]).astype(h_ref.dtype)


def _mod_norm(x, g, scale, shift, tm):
    m, d = x.shape
    per_row = scale.shape[0] != 1
    mod_spec = (pl.BlockSpec((tm, d), lambda i: (i, 0)) if per_row
                else pl.BlockSpec((1, d), lambda i: (0, 0)))
    return pl.pallas_call(
        _norm_kernel,
        out_shape=jax.ShapeDtypeStruct((m, d), BF16),
        grid=(m // tm,),
        in_specs=[pl.BlockSpec((tm, d), lambda i: (i, 0)),
                  pl.BlockSpec((1, d), lambda i: (0, 0)),
                  mod_spec, mod_spec],
        out_specs=pl.BlockSpec((tm, d), lambda i: (i, 0)),
        compiler_params=_cparams(("arbitrary",)),
        name="mod_norm",
    )(x, g.reshape(1, d), scale, shift)


def _proj_kernel(h_ref, w_ref, *refs, lead_tiles, lead_scale, head_norm, cast_scale):
    o_refs = refs[1:] if head_norm else refs
    acc = jnp.dot(h_ref[...], w_ref[...], preferred_element_type=F32)
    if lead_tiles:
        acc = acc * jnp.where(pl.program_id(1) < lead_tiles, lead_scale, 1.0)

    def emit(cols, y):
        for o_ref in o_refs:
            o_ref[:, cols] = y if o_ref.dtype == F32 else (y * cast_scale).astype(o_ref.dtype)

    if head_norm:
        g = refs[0][...]
        for c in range(acc.shape[1] // SB_HEAD_DIM):
            cols = slice(c * SB_HEAD_DIM, (c + 1) * SB_HEAD_DIM)
            blk = acc[:, cols]
            ms = jnp.mean(blk * blk, axis=-1, keepdims=True)
            emit(cols, blk * lax.rsqrt(ms + NORM_EPS) * g)
    else:
        emit(slice(None), acc)


def _proj(h, w, col0, n, out_dtypes, tm, tn, name, *, lead_cols=0, lead_scale=1.0,
          norm_gain=None, cast_scale=1.0):
    m, k = h.shape
    tn = min(tn, n)
    j0 = col0 // tn
    assert col0 % tn == 0 and n % tn == 0 and m % tm == 0 and lead_cols % tn == 0
    in_specs = [pl.BlockSpec((tm, k), lambda i, j: (i, 0)),
                pl.BlockSpec((k, tn), lambda i, j: (0, j0 + j))]
    args = [h, w]
    if norm_gain is not None:
        in_specs.append(pl.BlockSpec((1, SB_HEAD_DIM), lambda i, j: (0, 0)))
        args.append(norm_gain.reshape(1, SB_HEAD_DIM))
    tile = pl.BlockSpec((tm, tn), lambda i, j: (i, j))
    return pl.pallas_call(
        functools.partial(_proj_kernel, lead_tiles=lead_cols // tn, lead_scale=lead_scale,
                          head_norm=norm_gain is not None, cast_scale=cast_scale),
        out_shape=tuple(jax.ShapeDtypeStruct((m, n), dt) for dt in out_dtypes),
        grid=(m // tm, n // tn),
        in_specs=in_specs,
        out_specs=tuple(tile for _ in out_dtypes),
        compiler_params=_cparams(("arbitrary", "arbitrary")),
        name=name,
    )(*args)


def _merge_kernel(oa_ref, ob_ref, wa_ref, wb_ref, ga_ref, gb_ref, y_ref):
    ya = jnp.dot(oa_ref[...], wa_ref[...], preferred_element_type=F32)
    yb = jnp.dot(ob_ref[...], wb_ref[...], preferred_element_type=F32)
    y_ref[...] = (_sigmoid(ga_ref[...]) * ya + _sigmoid(gb_ref[...]) * yb).astype(y_ref.dtype)


def _merge(oa, ob, wa, wb, uc, ga_col, gb_col, tm, tn):
    m, k = oa.shape
    n = wa.shape[1]
    row = lambda i, j: (i, 0)
    col = lambda i, j: (0, j)
    ja, jb = ga_col // tn, gb_col // tn
    return pl.pallas_call(
        _merge_kernel,
        out_shape=jax.ShapeDtypeStruct((m, n), BF16),
        grid=(m // tm, n // tn),
        in_specs=[pl.BlockSpec((tm, k), row), pl.BlockSpec((tm, k), row),
                  pl.BlockSpec((k, tn), col), pl.BlockSpec((k, tn), col),
                  pl.BlockSpec((tm, tn), lambda i, j: (i, ja + j)),
                  pl.BlockSpec((tm, tn), lambda i, j: (i, jb + j))],
        out_specs=pl.BlockSpec((tm, tn), lambda i, j: (i, j)),
        compiler_params=_cparams(("arbitrary", "arbitrary")),
        name="merge",
    )(oa, ob, wa, wb, uc, uc)


def _out_kernel(y_ref, w_ref, x_ref, gate_ref, o_ref):
    acc = jnp.dot(y_ref[...], w_ref[...], preferred_element_type=F32)
    o_ref[...] = x_ref[...] + gate_ref[...] * acc


def _out_proj(y, w, x, gate, tm, tn):
    m, k = y.shape
    n = w.shape[1]
    per_row = gate.shape[0] != 1
    gate_spec = (pl.BlockSpec((tm, tn), lambda i, j: (i, j)) if per_row
                 else pl.BlockSpec((1, tn), lambda i, j: (0, j)))
    return pl.pallas_call(
        _out_kernel,
        out_shape=jax.ShapeDtypeStruct((m, n), F32),
        grid=(m // tm, n // tn),
        in_specs=[pl.BlockSpec((tm, k), lambda i, j: (i, 0)),
                  pl.BlockSpec((k, tn), lambda i, j: (0, j)),
                  pl.BlockSpec((tm, tn), lambda i, j: (i, j)),
                  gate_spec],
        out_specs=pl.BlockSpec((tm, tn), lambda i, j: (i, j)),
        compiler_params=_cparams(("arbitrary", "arbitrary")),
        name="out_proj",
    )(y, w, x, gate)


GLA_SUB = 16
GLA_ROWS = 256


def _split3(x):
    hi = x.astype(BF16)
    r1 = x - hi.astype(F32)
    mid = r1.astype(BF16)
    lo = (r1 - mid.astype(F32)).astype(BF16)
    return hi, mid, lo


def _gla_prompt_kernel(q_ref, k_ref, v_ref, z_ref, glr_ref, wg_ref, bg_ref, gn_ref, tri_ref,
                       o_ref, s_ref, st_ref, b_ref):
    t = pl.program_id(0)
    nt = pl.num_programs(0)
    rows = q_ref.shape[0]
    nh, dv, dk = st_ref.shape
    c = GLA_SUB

    @pl.when(t == 0)
    def _():
        st_ref[...] = jnp.zeros_like(st_ref)

    x = jnp.dot(glr_ref[...].astype(BF16), wg_ref[...], preferred_element_type=F32) + bg_ref[...]
    la = (jnp.minimum(x, 0.0) - jnp.log(1.0 + jnp.exp(-jnp.abs(x)))) * (1.0 / GLA_TAU)
    tri = tri_ref[...]
    hi, mid, lo = _split3(la)
    b_ref[...] = (jnp.dot(tri, hi, preferred_element_type=F32)
                  + jnp.dot(tri, mid, preferred_element_type=F32)
                  + jnp.dot(tri, lo, preferred_element_type=F32))

    row_id = lax.broadcasted_iota(jnp.int32, (c // 2, 1), 0)
    lane_id = lax.broadcasted_iota(jnp.int32, (c // 2, c), 1)
    gn = gn_ref[...]

    def chunk(ci, carry):
        r0 = pl.multiple_of(ci * c, c)
        rs = pl.ds(r0, c)
        heads = range(nh)
        ksl = [slice(h * dk, (h + 1) * dk) for h in heads]
        vsl = [slice(h * dv, (h + 1) * dv) for h in heads]
        qc = [q_ref[rs, ksl[h]] for h in heads]
        bc = [b_ref[rs, ksl[h]] for h in heads]
        b_end = [b_ref[pl.ds(r0 + c - 1, 1), ksl[h]] for h in heads]
        st = [st_ref[h] for h in heads]
        o_st = [lax.dot_general((qc[h] * jnp.exp(bc[h])).astype(BF16), st[h].astype(BF16),
                                (((1,), (1,)), ((), ())), preferred_element_type=F32)
                for h in heads]
        upd = [lax.dot_general(v_ref[rs, vsl[h]].astype(BF16),
                               (k_ref[rs, ksl[h]] * jnp.exp(b_end[h] - bc[h])).astype(BF16),
                               (((0,), (0,)), ((), ())), preferred_element_type=F32)
               for h in heads]
        half = c // 2
        a_top = [jnp.zeros((half, c), F32) for _ in heads]
        a_bot = [jnp.zeros((half, c), F32) for _ in heads]
        for s in range(c):
            for h in heads:
                ks = k_ref[pl.ds(r0 + s, 1), ksl[h]]
                bs = b_ref[pl.ds(r0 + s, 1), ksl[h]]
                if s < half:
                    w = jnp.where(row_id >= s,
                                  qc[h][:half] * ks * jnp.exp(bc[h][:half] - bs), 0.0)
                    a_top[h] = jnp.where(lane_id == s, jnp.sum(w, axis=-1, keepdims=True), a_top[h])
                    w = qc[h][half:] * ks * jnp.exp(bc[h][half:] - bs)
                else:
                    w = jnp.where(row_id >= s - half,
                                  qc[h][half:] * ks * jnp.exp(bc[h][half:] - bs), 0.0)
                a_bot[h] = jnp.where(lane_id == s, jnp.sum(w, axis=-1, keepdims=True), a_bot[h])
        for h in heads:
            a = jnp.concatenate([a_top[h], a_bot[h]], axis=0).astype(BF16)
            o = o_st[h] + jnp.dot(a, v_ref[rs, vsl[h]].astype(BF16), preferred_element_type=F32)
            ms = jnp.mean(o * o, axis=-1, keepdims=True)
            zc = z_ref[rs, vsl[h]]
            y = o * lax.rsqrt(ms + NORM_EPS) * gn * (zc * _sigmoid(zc))
            o_ref[rs, vsl[h]] = y.astype(o_ref.dtype)
            st_ref[h] = st[h] * jnp.exp(b_end[h]) + upd[h]
        return carry

    lax.fori_loop(0, rows // c, chunk, 0)

    @pl.when(t == nt - 1)
    def _():
        for h in range(nh):
            s_ref[h] = st_ref[h].T


def _gla_prompt(ua, glr, wg, bg, gn, dk, dv):
    t_len = ua.shape[0]
    h = GLA_HEADS
    rows = min(GLA_ROWS, t_len)
    r = np.arange(rows)
    tri = ((r[:, None] >= r[None, :]) & (r[:, None] // GLA_SUB == r[None, :] // GLA_SUB))
    tri = jnp.asarray(tri, dtype=BF16)
    v_blk = 2 * h * dk // (h * dv)
    return pl.pallas_call(
        _gla_prompt_kernel,
        out_shape=(jax.ShapeDtypeStruct((t_len, h * dv), BF16),
                   jax.ShapeDtypeStruct((h, dk, dv), F32)),
        grid=(t_len // rows,),
        in_specs=[pl.BlockSpec((rows, h * dk), lambda tt: (tt, 0)),
                  pl.BlockSpec((rows, h * dk), lambda tt: (tt, 1)),
                  pl.BlockSpec((rows, h * dv), lambda tt: (tt, v_blk)),
                  pl.BlockSpec((rows, h * dv), lambda tt: (tt, v_blk + 1)),
                  pl.BlockSpec((rows, GLA_RANK), lambda tt: (tt, 0)),
                  pl.BlockSpec((GLA_RANK, h * dk), lambda tt: (0, 0)),
                  pl.BlockSpec((1, h * dk), lambda tt: (0, 0)),
                  pl.BlockSpec((1, dv), lambda tt: (0, 0)),
                  pl.BlockSpec((rows, rows), lambda tt: (0, 0))],
        out_specs=(pl.BlockSpec((rows, h * dv), lambda tt: (tt, 0)),
                   pl.BlockSpec((h, dk, dv), lambda tt: (0, 0, 0))),
        scratch_shapes=[pltpu.VMEM((h, dv, dk), F32), pltpu.VMEM((rows, h * dk), F32)],
        compiler_params=_cparams(("arbitrary",)),
        name="gla_prompt",
    )(ua, ua, ua, ua, glr, wg, bg.reshape(1, -1), gn.reshape(1, -1), tri)


GLA_PAD_ROWS = 8


def _gla_sample_kernel(q_ref, k_ref, v_ref, z_ref, glr_ref, wg_ref, bg_ref, gn_ref, s_ref,
                       o_ref, so_ref, *, n_tok):
    rows = q_ref.shape[1]
    dk = s_ref.shape[2]
    dv = s_ref.shape[3]
    row_id = lax.broadcasted_iota(jnp.int32, (rows, 1), 0)
    gn = gn_ref[...]
    glr = glr_ref[0].astype(BF16)
    for h in range(GLA_HEADS):
        ksl = slice(h * dk, (h + 1) * dk)
        vsl = slice(h * dv, (h + 1) * dv)
        x = jnp.dot(glr, wg_ref[:, ksl], preferred_element_type=F32) + bg_ref[:, ksl]
        la = (jnp.minimum(x, 0.0) - jnp.log(1.0 + jnp.exp(-jnp.abs(x)))) * (1.0 / GLA_TAU)
        bc = jnp.zeros_like(la)
        for s in range(n_tok):
            bc = bc + jnp.where(row_id >= s, la[s:s + 1, :], 0.0)
        b_end = bc[n_tok - 1:n_tok, :]
        qc = q_ref[0, :, ksl]
        kc = k_ref[0, :, ksl]
        vc = v_ref[0, :, vsl]
        st = s_ref[0, h].T
        qd = (qc * jnp.exp(bc)).astype(BF16)
        o = lax.dot_general(qd, st.astype(BF16), (((1,), (1,)), ((), ())),
                            preferred_element_type=F32)
        for s in range(n_tok):
            w = jnp.where(row_id >= s, qc * kc[s:s + 1, :] * jnp.exp(bc - bc[s:s + 1, :]), 0.0)
            o = o + jnp.sum(w, axis=-1, keepdims=True) * vc[s:s + 1, :]
        ms = jnp.mean(o * o, axis=-1, keepdims=True)
        zc = z_ref[0, :, vsl]
        o_ref[0, :, vsl] = o * lax.rsqrt(ms + NORM_EPS) * gn * (zc * _sigmoid(zc))
        kd = (kc * jnp.exp(b_end - bc)).astype(BF16)
        upd = lax.dot_general(vc.astype(BF16), kd, (((0,), (0,)), ((), ())),
                              preferred_element_type=F32)
        so_ref[0, h] = (st * jnp.exp(b_end) + upd).T


def _gla_sample(ua, glr, wg, bg, gn, state, n_tok):
    b, rows, _ = ua.shape
    h, dk, dv = state.shape[1:]
    k_blk = h * dk // (h * dk)
    v_blk = 2 * h * dk // (h * dv)
    z_blk = v_blk + 1
    return pl.pallas_call(
        functools.partial(_gla_sample_kernel, n_tok=n_tok),
        out_shape=(jax.ShapeDtypeStruct((b, rows, h * dv), F32),
                   jax.ShapeDtypeStruct(state.shape, F32)),
        grid=(b,),
        in_specs=[pl.BlockSpec((1, rows, h * dk), lambda i: (i, 0, 0)),
                  pl.BlockSpec((1, rows, h * dk), lambda i: (i, 0, k_blk)),
                  pl.BlockSpec((1, rows, h * dv), lambda i: (i, 0, v_blk)),
                  pl.BlockSpec((1, rows, h * dv), lambda i: (i, 0, z_blk)),
                  pl.BlockSpec((1, rows, GLA_RANK), lambda i: (i, 0, 0)),
                  pl.BlockSpec((GLA_RANK, h * dk), lambda i: (0, 0)),
                  pl.BlockSpec((1, h * dk), lambda i: (0, 0)),
                  pl.BlockSpec((1, dv), lambda i: (0, 0)),
                  pl.BlockSpec((1, h, dk, dv), lambda i: (i, 0, 0, 0))],
        out_specs=(pl.BlockSpec((1, rows, h * dv), lambda i: (i, 0, 0)),
                   pl.BlockSpec((1, h, dk, dv), lambda i: (i, 0, 0, 0))),
        compiler_params=_cparams(("arbitrary",)),
        name="gla_sample",
    )(ua, ua, ua, ua, glr, wg, bg.reshape(1, -1), gn.reshape(1, -1), state)


SB_BLK = 1024
SB_SUB = 128


def _suffix_matrix():
    j = np.arange(SB_SUB)
    u = (j[:, None] >= j[None, :]).astype(np.float32)
    half = np.concatenate([u, np.ones((SB_SUB, SB_SUB), np.float32)], axis=1)
    return jnp.asarray(np.concatenate([half, half], axis=0), dtype=BF16)


def _sb_suffix(z, uw, mask):
    neg_abs = lax.bitcast_convert_type(
        lax.bitcast_convert_type(z, jnp.int32) | jnp.int32(-2 ** 31), F32)
    sp = jnp.maximum(z, 0.0) + jnp.log(1.0 + jnp.exp2(neg_abs)) * LOG2E
    if mask is not None:
        sp = jnp.where(mask, sp, 0.0)
    hi = sp.astype(BF16)
    lo = (sp - hi.astype(F32)).astype(BF16)
    return jnp.dot(jnp.concatenate([hi, lo], axis=1), uw, preferred_element_type=F32)


def _sb_weights(z, s2, car, mask):
    a = jnp.exp2(z - s2[:, :SB_SUB] - car)
    if mask is not None:
        a = jnp.where(mask, a, 0.0)
    return a, car + s2[:, SB_SUB:]


def _sb_prompt_kernel(qi_ref, kj_ref, q_ref, k_ref, v_ref, z_ref, bias_ref, uw_ref,
                      o_ref, acc_ref, car_ref):
    step = pl.program_id(1)
    qi = qi_ref[step]
    kj = kj_ref[step]
    blk = q_ref.shape[0]
    nsub = blk // SB_SUB
    uw = uw_ref[...]
    bias = bias_ref[0]

    @pl.when(kj == qi)
    def _():
        acc_ref[...] = jnp.zeros_like(acc_ref)
        car_ref[...] = jnp.zeros_like(car_ref)

    def logits(ks, r0):
        return lax.dot_general(q_ref[r0:blk, :], k_ref[ks * SB_SUB:(ks + 1) * SB_SUB, :],
                               (((1,), (1,)), ((), ())), preferred_element_type=F32) + bias

    def causal_mask(shape, masked):
        if not masked:
            return None
        return lax.broadcasted_iota(jnp.int32, shape, 1) < lax.broadcasted_iota(jnp.int32, shape, 0)

    def finish(ks, r0, z, s2, mask):
        a, car = _sb_weights(z, s2, car_ref[r0:blk, :], mask)
        car_ref[r0:blk, :] = car
        acc_ref[r0:blk, :] += jnp.dot(a.astype(BF16), v_ref[ks * SB_SUB:(ks + 1) * SB_SUB, :],
                                      preferred_element_type=F32)

    def sweep(masked):
        tiles = [(ks, ks * SB_SUB if masked else 0) for ks in range(nsub - 1, -1, -1)]
        z = logits(*tiles[0])
        mask = causal_mask(z.shape, masked)
        s2 = _sb_suffix(z, uw, mask)
        for i, (ks, r0) in enumerate(tiles):
            nxt = tiles[i + 1] if i + 1 < len(tiles) else None
            if nxt is not None:
                z_n = logits(*nxt)
                mask_n = causal_mask(z_n.shape, masked)
            finish(ks, r0, z, s2, mask)
            if nxt is not None:
                s2 = _sb_suffix(z_n, uw, mask_n)
                z, mask = z_n, mask_n

    @pl.when(kj == qi)
    def _():
        sweep(True)

    @pl.when(kj != qi)
    def _():
        sweep(False)

    @pl.when(kj == 0)
    def _():
        zz = z_ref[...]
        o_ref[...] = (acc_ref[...] * (zz * _sigmoid(zz))).astype(o_ref.dtype)


def _sb_prompt(q16, k16, v16, uc, bias):
    t_len, w = q16.shape
    heads = w // SB_HEAD_DIM
    blk = min(SB_BLK, t_len)
    nq = t_len // blk
    qi = np.concatenate([np.full(i + 1, i) for i in range(nq)]).astype(np.int32)
    kj = np.concatenate([np.arange(i, -1, -1) for i in range(nq)]).astype(np.int32)
    bias_b = jnp.broadcast_to((bias.astype(F32) * LOG2E)[:, None, None], (heads, 1, SB_HEAD_DIM))
    qmap = lambda h, s, qi_r, kj_r: (qi_r[s], h)
    kmap = lambda h, s, qi_r, kj_r: (kj_r[s], h)
    return pl.pallas_call(
        _sb_prompt_kernel,
        out_shape=jax.ShapeDtypeStruct((t_len, w), BF16),
        grid_spec=pltpu.PrefetchScalarGridSpec(
            num_scalar_prefetch=2,
            grid=(heads, len(qi)),
            in_specs=[pl.BlockSpec((blk, SB_HEAD_DIM), qmap),
                      pl.BlockSpec((blk, SB_HEAD_DIM), kmap),
                      pl.BlockSpec((blk, SB_HEAD_DIM), kmap),
                      pl.BlockSpec((blk, SB_HEAD_DIM), qmap),
                      pl.BlockSpec((1, 1, SB_HEAD_DIM), lambda h, s, qi_r, kj_r: (h, 0, 0)),
                      pl.BlockSpec((2 * SB_SUB, 2 * SB_SUB), lambda h, s, qi_r, kj_r: (0, 0))],
            out_specs=pl.BlockSpec((blk, SB_HEAD_DIM), qmap),
            scratch_shapes=[pltpu.VMEM((blk, SB_HEAD_DIM), F32),
                            pltpu.VMEM((blk, SB_HEAD_DIM), F32)]),
        compiler_params=_cparams(("arbitrary", "arbitrary")),
        name="sb_prompt",
    )(jnp.asarray(qi), jnp.asarray(kj), q16, k16, v16, uc, bias_b, _suffix_matrix())


SB_PAIR = 2
SB_SEQS = 4


def _sb_sample_kernel(pt_ref, q_ref, kc_ref, vc_ref, kn_ref, vn_ref, z_ref, bias_ref, uw_ref,
                      o_ref, kbuf, vbuf, sem, acc_ref, car_ref, kpg_ref, vpg_ref, *, n_tok, n_pages):
    n_seq, heads, rows, _ = q_ref.shape
    nsq = kbuf.shape[1]
    page_len = kbuf.shape[3]
    ngrp = heads // SB_PAIR
    total = (n_seq // nsq) * n_pages
    uw = uw_ref[...]
    bias = bias_ref[...]

    def page_copies(i, slot):
        i = jnp.minimum(i, total - 1)
        b0 = (i // n_pages) * nsq
        p = i % n_pages
        out = []
        for s in range(nsq):
            pg = pt_ref[(b0 + s) * n_pages + (n_pages - 1 - p)]
            for h in range(heads):
                out.append(pltpu.make_async_copy(kc_ref.at[pg, :, h, :], kbuf.at[slot, s, h],
                                                 sem.at[0, slot]))
                out.append(pltpu.make_async_copy(vc_ref.at[pg, :, h, :], vbuf.at[slot, s, h],
                                                 sem.at[1, slot]))
        return out

    def pages(b0, k_head, v_head, masked):
        zs = []
        for s in range(nsq):
            for g in range(ngrp):
                zg = None
                for h in range(g * SB_PAIR, (g + 1) * SB_PAIR):
                    zh = lax.dot_general(q_ref[b0 + s, h], k_head(s, h), (((1,), (1,)), ((), ())),
                                         preferred_element_type=F32)
                    zg = zh if zg is None else zg + zh
                zs.append(zg)
        z = jnp.concatenate(zs, axis=0) + bias
        mask = None
        if masked:
            tok = lax.broadcasted_iota(jnp.int32, z.shape, 0) % n_tok
            mask = lax.broadcasted_iota(jnp.int32, z.shape, 1) < tok
        a, car = _sb_weights(z, _sb_suffix(z, uw, mask), car_ref[...], mask)
        car_ref[...] = car
        a16 = a.astype(BF16)
        for s in range(nsq):
            for h in range(heads):
                r0 = (s * ngrp + h // SB_PAIR) * rows
                acc_ref[s, h] += jnp.dot(a16[r0:r0 + rows, :], v_head(s, h),
                                         preferred_element_type=F32)

    kpg_ref[...] = jnp.zeros_like(kpg_ref)
    vpg_ref[...] = jnp.zeros_like(vpg_ref)
    for c in page_copies(0, 0):
        c.start()

    def step(i, carry):
        slot = i % 2
        b0 = (i // n_pages) * nsq
        p = i % n_pages
        for c in page_copies(i + 1, 1 - slot):
            c.start()

        @pl.when(p == 0)
        def _():
            acc_ref[...] = jnp.zeros_like(acc_ref)
            car_ref[...] = jnp.zeros_like(car_ref)
            for s in range(nsq):
                kpg_ref[s, 0:kn_ref.shape[1], :] = kn_ref[b0 + s]
                vpg_ref[s, 0:vn_ref.shape[1], :] = vn_ref[b0 + s]
            pages(b0,
                  lambda s, h: kpg_ref[s, pl.ds(h, page_len, stride=heads), :].astype(BF16),
                  lambda s, h: vpg_ref[s, pl.ds(h, page_len, stride=heads), :].astype(BF16), True)

        for c in page_copies(i, slot):
            c.wait()
        pages(b0, lambda s, h: kbuf[slot, s, h].astype(BF16),
              lambda s, h: vbuf[slot, s, h].astype(BF16), False)

        @pl.when(p == n_pages - 1)
        def _():
            first = lax.broadcasted_iota(jnp.int32, (rows, SB_HEAD_DIM), 0) < n_tok
            for s in range(nsq):
                for g in range(ngrp):
                    og = jnp.where(first, acc_ref[s, g * SB_PAIR], acc_ref[s, g * SB_PAIR + 1])
                    zz = z_ref[b0 + s, g]
                    o_ref[b0 + s, g] = og * (zz * _sigmoid(zz))
        return carry

    lax.fori_loop(0, total, step, 0)
    for c in page_copies(total, total % 2):
        c.wait()


def _sb_sample(page_table, q8, cache_k, cache_v, k_new, v_new, z_g, bias_rows, n_tok):
    b, heads, rows, _ = q8.shape
    ngrp = heads // SB_PAIR
    n_pages = page_table.shape[1]
    page = cache_k.shape[1]
    nsq = SB_SEQS
    assert b % nsq == 0
    vmem = pl.BlockSpec(memory_space=pltpu.VMEM)
    hbm = pl.BlockSpec(memory_space=pl.ANY)
    return pl.pallas_call(
        functools.partial(_sb_sample_kernel, n_tok=n_tok, n_pages=n_pages),
        out_shape=jax.ShapeDtypeStruct((b, ngrp, rows, SB_HEAD_DIM), F32),
        in_specs=[pl.BlockSpec(memory_space=pltpu.SMEM), vmem, hbm, hbm,
                  vmem, vmem, vmem, vmem, vmem],
        out_specs=vmem,
        scratch_shapes=[pltpu.VMEM((2, nsq, heads, page, SB_HEAD_DIM), F32),
                        pltpu.VMEM((2, nsq, heads, page, SB_HEAD_DIM), F32),
                        pltpu.SemaphoreType.DMA((2, 2)),
                        pltpu.VMEM((nsq, heads, rows, SB_HEAD_DIM), F32),
                        pltpu.VMEM((nsq * ngrp * rows, SB_HEAD_DIM), F32),
                        pltpu.VMEM((nsq, page * heads, SB_HEAD_DIM), F32),
                        pltpu.VMEM((nsq, page * heads, SB_HEAD_DIM), F32)],
        compiler_params=pltpu.CompilerParams(vmem_limit_bytes=VMEM_LIMIT),
        name="sb_sample",
    )(page_table.reshape(-1), q8, cache_k, cache_v, k_new, v_new, z_g,
      jnp.tile(bias_rows, (nsq, 1)), _suffix_matrix())


def _pad_rows(a, rows):
    return jnp.pad(a, ((0, 0), (0, rows - a.shape[1]), (0, 0)))


def _layer(xp, xs, c_all, w, cache_k, cache_v, state, page_table, n_seq, n_tok):
    (w_ada, b_ada, norm_g, w_in, w_gate_up, b_gate, gla_norm_g,
     q_norm_g, k_norm_g, sb_bias, w_branch_a, w_branch_b, w_out) = w
    d = xp.shape[1]
    heads = d // SB_HEAD_DIM
    dv = gla_norm_g.shape[0]
    key_w = w_gate_up.shape[1]
    dk = key_w // GLA_HEADS
    val_w = GLA_HEADS * dv
    sb_w = heads * SB_HEAD_DIM

    c0 = 2 * key_w + 2 * val_w
    c1 = c0 + GLA_RANK
    w_a = w_in[:, :c0].astype(BF16)
    w_glr = w_in[:, c0:c1].astype(BF16)
    w_b = w_in[:, c1:].astype(BF16)
    wa16, wb16, wo16 = (w_branch_a.astype(BF16), w_branch_b.astype(BF16), w_out.astype(BF16))
    wg16 = w_gate_up.astype(BF16)

    mod = _ada(c_all, w_ada, b_ada)
    shift, scale, gate = mod[:, :d], mod[:, d:2 * d], mod[:, 2 * d:]
    rep = lambda a: jnp.repeat(a[:n_seq], n_tok, axis=0)
    q_scale = LOG2E / math.sqrt(SB_HEAD_DIM)

    def in_proj(h, tm, want_16):
        kv_types = (F32, BF16) if want_16 else (F32,)
        (ua,) = _proj(h, w_a, 0, c0, (F32,), tm, 1024, "proj_gla",
                      lead_cols=key_w, lead_scale=dk ** -0.5)
        (glr,) = _proj(h, w_glr, 0, GLA_RANK, (F32,), tm, GLA_RANK, "proj_glr")
        (q16,) = _proj(h, w_b, 0, sb_w, (BF16,), tm, 1024, "proj_qb",
                       norm_gain=q_norm_g, cast_scale=q_scale)
        k_out = _proj(h, w_b, sb_w, sb_w, kv_types, tm, 1024, "proj_kb", norm_gain=k_norm_g)
        v_out = _proj(h, w_b, 2 * sb_w, sb_w, kv_types, tm, 1024, "proj_vb")
        (uc,) = _proj(h, w_b, 3 * sb_w, sb_w + 2 * d, (F32,), tm, 1024, "proj_gates")
        return ua, glr, q16, k_out, v_out, uc

    hp = _mod_norm(xp, norm_g, scale[n_seq:n_seq + 1], shift[n_seq:n_seq + 1], 512)
    ua, glr, q16, (k32, k16), (v32, v16), uc = in_proj(hp, 1024, True)
    oa, s_p = _gla_prompt(ua, glr, wg16, b_gate, gla_norm_g, dk, dv)
    ob = _sb_prompt(q16, k16, v16, uc, sb_bias)
    y = _merge(oa, ob, wa16, wb16, uc, sb_w, sb_w + d, 512, 1024)
    out_p = _out_proj(y, wo16, xp, gate[n_seq:n_seq + 1], 1024, 1024)
    kp, vp = k32, v32

    m_s = xs.shape[0]
    hs = _mod_norm(xs, norm_g, rep(scale), rep(shift), m_s)
    ua, glr, q16, (k32,), (v32,), uc = in_proj(hs, m_s, False)
    ua3 = _pad_rows(ua.reshape(n_seq, n_tok, -1), GLA_PAD_ROWS)
    glr3 = _pad_rows(glr.reshape(n_seq, n_tok, -1), GLA_PAD_ROWS)
    oa3, s_s = _gla_sample(ua3, glr3, wg16, b_gate, gla_norm_g, state, n_tok)
    oa = oa3[:, :n_tok].reshape(m_s, val_w).astype(BF16)

    ngrp = heads // SB_PAIR
    qg = q16.reshape(n_seq, n_tok, ngrp, SB_PAIR, SB_HEAD_DIM).transpose(0, 2, 3, 1, 4)
    eye = jnp.eye(SB_PAIR, dtype=BF16)[None, None, :, :, None, None]
    q8 = (qg[:, :, :, None, :, :] * eye).reshape(n_seq, heads, SB_PAIR * n_tok, SB_HEAD_DIM)
    z_g = (uc[:, :sb_w].reshape(n_seq, n_tok, ngrp, SB_PAIR, SB_HEAD_DIM)
           .transpose(0, 2, 3, 1, 4).reshape(n_seq, ngrp, SB_PAIR * n_tok, SB_HEAD_DIM))
    bias_rows = jnp.broadcast_to(jnp.repeat(sb_bias.astype(F32) * LOG2E, n_tok)[:, None],
                                 (heads * n_tok, SB_HEAD_DIM))
    k_new = k32.reshape(n_seq, n_tok * heads, SB_HEAD_DIM)
    v_new = v32.reshape(n_seq, n_tok * heads, SB_HEAD_DIM)
    ob_g = _sb_sample(page_table, q8, cache_k, cache_v, k_new, v_new, z_g, bias_rows, n_tok)
    ob = (ob_g.reshape(n_seq, ngrp, SB_PAIR, n_tok, SB_HEAD_DIM).transpose(0, 3, 1, 2, 4)
          .reshape(m_s, sb_w).astype(BF16))
    y = _merge(oa, ob, wa16, wb16, uc, sb_w, sb_w + d, m_s, 1024)
    out_s = _out_proj(y, wo16, xs, rep(gate), m_s, 1024)
    return out_p, out_s, kp, vp, s_p, k32, v32, s_s


def kernel(x_prompt, x_sample, cache_k, cache_v, state_gla, page_table, c_prompt, c_sample,
           w_ada, b_ada, norm_g, w_in, w_gate_up, b_gate, gla_norm_g, q_norm_g, k_norm_g,
           sb_bias, w_branch_a, w_branch_b, w_out):
    depth = w_ada.shape[0]
    n_b, t_len, d = x_prompt.shape
    assert n_b == 1, "the prompt group is a single sequence"
    n_seq, n_tok, _ = x_sample.shape
    heads = d // SB_HEAD_DIM
    pad = (-(n_seq + n_b)) % 8
    c_all = jnp.concatenate([c_sample, c_prompt, jnp.zeros((pad, d), F32)], axis=0)
    xp = x_prompt.reshape(t_len, d)
    xs = x_sample.reshape(n_seq * n_tok, d)
    outs = [[] for _ in range(6)]
    for l in range(depth):
        w = (w_ada[l], b_ada[l], norm_g[l], w_in[l], w_gate_up[l], b_gate[l], gla_norm_g[l],
             q_norm_g[l], k_norm_g[l], sb_bias[l], w_branch_a[l], w_branch_b[l], w_out[l])
        xp, xs, kp, vp, sp, ks, vs, ss = _layer(xp, xs, c_all, w, cache_k[l], cache_v[l],
                                                 state_gla[l], page_table, n_seq, n_tok)
        outs[0].append(kp.reshape(n_b, t_len, heads, SB_HEAD_DIM))
        outs[1].append(vp.reshape(n_b, t_len, heads, SB_HEAD_DIM))
        outs[2].append(sp[None])
        outs[3].append(ks.reshape(n_seq, n_tok, heads, SB_HEAD_DIM))
        outs[4].append(vs.reshape(n_seq, n_tok, heads, SB_HEAD_DIM))
        outs[5].append(ss)
    k_p, v_p, s_p, k_s, v_s, s_s = (jnp.stack(o) for o in outs)
    return (xp.reshape(n_b, t_len, d), xs.reshape(n_seq, n_tok, d), k_p, v_p, s_p, k_s, v_s, s_s)
```

```python
import functools
import math

import numpy as np
import jax
import jax.numpy as jnp
from jax import lax
from jax.experimental import pallas as pl
from jax.experimental.pallas import tpu as pltpu

F32 = jnp.float32
BF16 = jnp.bfloat16

GLA_HEADS = 4
GLA_RANK = 16
GLA_TAU = 16.0
SB_HEAD_DIM = 128
NORM_EPS = 1e-6
LOG2E = 1.4426950408889634
LN2 = 0.6931471805599453

LANES = 128
VMEM_LIMIT = 56 * 1024 * 1024


def _cparams(sem):
    return pltpu.CompilerParams(dimension_semantics=sem, vmem_limit_bytes=VMEM_LIMIT)


def _sigmoid(x):
    return 1.0 / (1.0 + jnp.exp(-x))


def _softplus_parts(z):
    l = jnp.log(1.0 + jnp.exp(-jnp.abs(z)))
    return jnp.maximum(z, 0.0) + l, l


def _ada_kernel(c_ref, w_ref, b_ref, o_ref):
    c = c_ref[...]
    s = (c * _sigmoid(c)).astype(BF16)
    o_ref[...] = jnp.dot(s, w_ref[...].astype(BF16), preferred_element_type=F32) + b_ref[...]


def _ada(c_all, w_ada, b_ada, tn=512):
    r, d = c_all.shape
    n = w_ada.shape[1]
    return pl.pallas_call(
        _ada_kernel,
        out_shape=jax.ShapeDtypeStruct((r, n), F32),
        grid=(n // tn,),
        in_specs=[pl.BlockSpec((r, d), lambda j: (0, 0)),
                  pl.BlockSpec((d, tn), lambda j: (0, j)),
                  pl.BlockSpec((1, tn), lambda j: (0, j))],
        out_specs=pl.BlockSpec((r, tn), lambda j: (0, j)),
        compiler_params=_cparams(("arbitrary",)),
        name="ada_mod",
    )(c_all, w_ada, b_ada.reshape(1, n))


def _norm_kernel(x_ref, g_ref, scale_ref, shift_ref, h_ref):
    x = x_ref[...]
    ms = jnp.mean(x * x, axis=-1, keepdims=True)
    y = x * lax.rsqrt(ms + NORM_EPS) * g_ref[...]
    h_ref[...] = (y * (1.0 + scale_ref[...]) + shift_ref[...]).astype(h_ref.dtype)


def _mod_norm(x, g, scale, shift, tm):
    m, d = x.shape
    per_row = scale.shape[0] != 1
    mod_spec = (pl.BlockSpec((tm, d), lambda i: (i, 0)) if per_row
                else pl.BlockSpec((1, d), lambda i: (0, 0)))
    return pl.pallas_call(
        _norm_kernel,
        out_shape=jax.ShapeDtypeStruct((m, d), BF16),
        grid=(m // tm,),
        in_specs=[pl.BlockSpec((tm, d), lambda i: (i, 0)),
                  pl.BlockSpec((1, d), lambda i: (0, 0)),
                  mod_spec, mod_spec],
        out_specs=pl.BlockSpec((tm, d), lambda i: (i, 0)),
        compiler_params=_cparams(("arbitrary",)),
        name="mod_norm",
    )(x, g.reshape(1, d), scale, shift)


def _proj_kernel(h_ref, w_ref, *refs, lead_tiles, lead_scale, head_norm, cast_scale):
    o_refs = refs[1:] if head_norm else refs
    acc = jnp.dot(h_ref[...], w_ref[...], preferred_element_type=F32)
    if lead_tiles:
        acc = acc * jnp.where(pl.program_id(1) < lead_tiles, lead_scale, 1.0)

    def emit(cols, y):
        for o_ref in o_refs:
            o_ref[:, cols] = y if o_ref.dtype == F32 else (y * cast_scale).astype(o_ref.dtype)

    if head_norm:
        g = refs[0][...]
        for c in range(acc.shape[1] // SB_HEAD_DIM):
            cols = slice(c * SB_HEAD_DIM, (c + 1) * SB_HEAD_DIM)
            blk = acc[:, cols]
            ms = jnp.mean(blk * blk, axis=-1, keepdims=True)
            emit(cols, blk * lax.rsqrt(ms + NORM_EPS) * g)
    else:
        emit(slice(None), acc)


def _proj(h, w, col0, n, out_dtypes, tm, tn, name, *, lead_cols=0, lead_scale=1.0,
          norm_gain=None, cast_scale=1.0):
    m, k = h.shape
    tn = min(tn, n)
    j0 = col0 // tn
    assert col0 % tn == 0 and n % tn == 0 and m % tm == 0 and lead_cols % tn == 0
    in_specs = [pl.BlockSpec((tm, k), lambda i, j: (i, 0)),
                pl.BlockSpec((k, tn), lambda i, j: (0, j0 + j))]
    args = [h, w]
    if norm_gain is not None:
        in_specs.append(pl.BlockSpec((1, SB_HEAD_DIM), lambda i, j: (0, 0)))
        args.append(norm_gain.reshape(1, SB_HEAD_DIM))
    tile = pl.BlockSpec((tm, tn), lambda i, j: (i, j))
    return pl.pallas_call(
        functools.partial(_proj_kernel, lead_tiles=lead_cols // tn, lead_scale=lead_scale,
                          head_norm=norm_gain is not None, cast_scale=cast_scale),
        out_shape=tuple(jax.ShapeDtypeStruct((m, n), dt) for dt in out_dtypes),
        grid=(m // tm, n // tn),
        in_specs=in_specs,
        out_specs=tuple(tile for _ in out_dtypes),
        compiler_params=_cparams(("arbitrary", "arbitrary")),
        name=name,
    )(*args)


def _merge_kernel(oa_ref, ob_ref, wa_ref, wb_ref, ga_ref, gb_ref, y_ref):
    ya = jnp.dot(oa_ref[...], wa_ref[...], preferred_element_type=F32)
    yb = jnp.dot(ob_ref[...], wb_ref[...], preferred_element_type=F32)
    y_ref[...] = (_sigmoid(ga_ref[...]) * ya + _sigmoid(gb_ref[...]) * yb).astype(y_ref.dtype)


def _merge(oa, ob, wa, wb, uc, ga_col, gb_col, tm, tn):
    m, k = oa.shape
    n = wa.shape[1]
    row = lambda i, j: (i, 0)
    col = lambda i, j: (0, j)
    ja, jb = ga_col // tn, gb_col // tn
    return pl.pallas_call(
        _merge_kernel,
        out_shape=jax.ShapeDtypeStruct((m, n), BF16),
        grid=(m // tm, n // tn),
        in_specs=[pl.BlockSpec((tm, k), row), pl.BlockSpec((tm, k), row),
                  pl.BlockSpec((k, tn), col), pl.BlockSpec((k, tn), col),
                  pl.BlockSpec((tm, tn), lambda i, j: (i, ja + j)),
                  pl.BlockSpec((tm, tn), lambda i, j: (i, jb + j))],
        out_specs=pl.BlockSpec((tm, tn), lambda i, j: (i, j)),
        compiler_params=_cparams(("arbitrary", "arbitrary")),
        name="merge",
    )(oa, ob, wa, wb, uc, uc)


def _out_kernel(y_ref, w_ref, x_ref, gate_ref, o_ref):
    acc = jnp.dot(y_ref[...], w_ref[...], preferred_element_type=F32)
    o_ref[...] = x_ref[...] + gate_ref[...] * acc


def _out_proj(y, w, x, gate, tm, tn):
    m, k = y.shape
    n = w.shape[1]
    per_row = gate.shape[0] != 1
    gate_spec = (pl.BlockSpec((tm, tn), lambda i, j: (i, j)) if per_row
                 else pl.BlockSpec((1, tn), lambda i, j: (0, j)))
    return pl.pallas_call(
        _out_kernel,
        out_shape=jax.ShapeDtypeStruct((m, n), F32),
        grid=(m // tm, n // tn),
        in_specs=[pl.BlockSpec((tm, k), lambda i, j: (i, 0)),
                  pl.BlockSpec((k, tn), lambda i, j: (0, j)),
                  pl.BlockSpec((tm, tn), lambda i, j: (i, j)),
                  gate_spec],
        out_specs=pl.BlockSpec((tm, tn), lambda i, j: (i, j)),
        compiler_params=_cparams(("arbitrary", "arbitrary")),
        name="out_proj",
    )(y, w, x, gate)


GLA_SUB = 16
GLA_ROWS = 256


def _split3(x):
    hi = x.astype(BF16)
    r1 = x - hi.astype(F32)
    mid = r1.astype(BF16)
    lo = (r1 - mid.astype(F32)).astype(BF16)
    return hi, mid, lo


def _gla_prompt_kernel(q_ref, k_ref, v_ref, z_ref, glr_ref, wg_ref, bg_ref, gn_ref, tri_ref,
                       o_ref, s_ref, st_ref, b_ref):
    t = pl.program_id(0)
    nt = pl.num_programs(0)
    rows = q_ref.shape[0]
    nh, dv, dk = st_ref.shape
    c = GLA_SUB

    @pl.when(t == 0)
    def _():
        st_ref[...] = jnp.zeros_like(st_ref)

    x = jnp.dot(glr_ref[...].astype(BF16), wg_ref[...], preferred_element_type=F32) + bg_ref[...]
    la = (jnp.minimum(x, 0.0) - jnp.log(1.0 + jnp.exp(-jnp.abs(x)))) * (1.0 / GLA_TAU)
    tri = tri_ref[...]
    hi, mid, lo = _split3(la)
    b_ref[...] = (jnp.dot(tri, hi, preferred_element_type=F32)
                  + jnp.dot(tri, mid, preferred_element_type=F32)
                  + jnp.dot(tri, lo, preferred_element_type=F32))

    row_id = lax.broadcasted_iota(jnp.int32, (c // 2, 1), 0)
    lane_id = lax.broadcasted_iota(jnp.int32, (c // 2, c), 1)
    gn = gn_ref[...]

    def chunk(ci, carry):
        r0 = pl.multiple_of(ci * c, c)
        rs = pl.ds(r0, c)
        heads = range(nh)
        ksl = [slice(h * dk, (h + 1) * dk) for h in heads]
        vsl = [slice(h * dv, (h + 1) * dv) for h in heads]
        qc = [q_ref[rs, ksl[h]] for h in heads]
        bc = [b_ref[rs, ksl[h]] for h in heads]
        b_end = [b_ref[pl.ds(r0 + c - 1, 1), ksl[h]] for h in heads]
        st = [st_ref[h] for h in heads]
        o_st = [lax.dot_general((qc[h] * jnp.exp(bc[h])).astype(BF16), st[h].astype(BF16),
                                (((1,), (1,)), ((), ())), preferred_element_type=F32)
                for h in heads]
        upd = [lax.dot_general(v_ref[rs, vsl[h]].astype(BF16),
                               (k_ref[rs, ksl[h]] * jnp.exp(b_end[h] - bc[h])).astype(BF16),
                               (((0,), (0,)), ((), ())), preferred_element_type=F32)
               for h in heads]
        half = c // 2
        a_top = [jnp.zeros((half, c), F32) for _ in heads]
        a_bot = [jnp.zeros((half, c), F32) for _ in heads]
        for s in range(c):
            for h in heads:
                ks = k_ref[pl.ds(r0 + s, 1), ksl[h]]
                bs = b_ref[pl.ds(r0 + s, 1), ksl[h]]
                if s < half:
                    w = jnp.where(row_id >= s,
                                  qc[h][:half] * ks * jnp.exp(bc[h][:half] - bs), 0.0)
                    a_top[h] = jnp.where(lane_id == s, jnp.sum(w, axis=-1, keepdims=True), a_top[h])
                    w = qc[h][half:] * ks * jnp.exp(bc[h][half:] - bs)
                else:
                    w = jnp.where(row_id >= s - half,
                                  qc[h][half:] * ks * jnp.exp(bc[h][half:] - bs), 0.0)
                a_bot[h] = jnp.where(lane_id == s, jnp.sum(w, axis=-1, keepdims=True), a_bot[h])
        for h in heads:
            a = jnp.concatenate([a_top[h], a_bot[h]], axis=0).astype(BF16)
            o = o_st[h] + jnp.dot(a, v_ref[rs, vsl[h]].astype(BF16), preferred_element_type=F32)
            ms = jnp.mean(o * o, axis=-1, keepdims=True)
            zc = z_ref[rs, vsl[h]]
            y = o * lax.rsqrt(ms + NORM_EPS) * gn * (zc * _sigmoid(zc))
            o_ref[rs, vsl[h]] = y.astype(o_ref.dtype)
            st_ref[h] = st[h] * jnp.exp(b_end[h]) + upd[h]
        return carry

    lax.fori_loop(0, rows // c, chunk, 0)

    @pl.when(t == nt - 1)
    def _():
        for h in range(nh):
            s_ref[h] = st_ref[h].T


def _gla_prompt(ua, glr, wg, bg, gn, dk, dv):
    t_len = ua.shape[0]
    h = GLA_HEADS
    rows = min(GLA_ROWS, t_len)
    r = np.arange(rows)
    tri = ((r[:, None] >= r[None, :]) & (r[:, None] // GLA_SUB == r[None, :] // GLA_SUB))
    tri = jnp.asarray(tri, dtype=BF16)
    v_blk = 2 * h * dk // (h * dv)
    return pl.pallas_call(
        _gla_prompt_kernel,
        out_shape=(jax.ShapeDtypeStruct((t_len, h * dv), BF16),
                   jax.ShapeDtypeStruct((h, dk, dv), F32)),
        grid=(t_len // rows,),
        in_specs=[pl.BlockSpec((rows, h * dk), lambda tt: (tt, 0)),
                  pl.BlockSpec((rows, h * dk), lambda tt: (tt, 1)),
                  pl.BlockSpec((rows, h * dv), lambda tt: (tt, v_blk)),
                  pl.BlockSpec((rows, h * dv), lambda tt: (tt, v_blk + 1)),
                  pl.BlockSpec((rows, GLA_RANK), lambda tt: (tt, 0)),
                  pl.BlockSpec((GLA_RANK, h * dk), lambda tt: (0, 0)),
                  pl.BlockSpec((1, h * dk), lambda tt: (0, 0)),
                  pl.BlockSpec((1, dv), lambda tt: (0, 0)),
                  pl.BlockSpec((rows, rows), lambda tt: (0, 0))],
        out_specs=(pl.BlockSpec((rows, h * dv), lambda tt: (tt, 0)),
                   pl.BlockSpec((h, dk, dv), lambda tt: (0, 0, 0))),
        scratch_shapes=[pltpu.VMEM((h, dv, dk), F32), pltpu.VMEM((rows, h * dk), F32)],
        compiler_params=_cparams(("arbitrary",)),
        name="gla_prompt",
    )(ua, ua, ua, ua, glr, wg, bg.reshape(1, -1), gn.reshape(1, -1), tri)


GLA_PAD_ROWS = 8


def _log_decay(x):
    return (jnp.minimum(x, 0.0) - jnp.log(1.0 + jnp.exp(-jnp.abs(x)))) * (1.0 / GLA_TAU)


def _gla_sample_kernel(q_ref, k_ref, v_ref, z_ref, glr_ref, wg_ref, bg_ref, wgt_ref, bgt_ref,
                       gn_ref, s_ref, o_ref, so_ref, *, n_tok):
    rows = q_ref.shape[1]
    dk = s_ref.shape[2]
    dv = s_ref.shape[3]
    row_id = lax.broadcasted_iota(jnp.int32, (rows, 1), 0)
    real_tok = lax.broadcasted_iota(jnp.int32, (dk, rows), 1) < n_tok
    gn = gn_ref[...]
    glr = glr_ref[0].astype(BF16)
    for h in range(GLA_HEADS):
        ksl = slice(h * dk, (h + 1) * dk)
        vsl = slice(h * dv, (h + 1) * dv)
        la = _log_decay(jnp.dot(glr, wg_ref[:, ksl], preferred_element_type=F32) + bg_ref[:, ksl])
        bc = jnp.zeros_like(la)
        for s in range(n_tok):
            bc = bc + jnp.where(row_id >= s, la[s:s + 1, :], 0.0)
        b_end = bc[n_tok - 1:n_tok, :]
        la_t = _log_decay(lax.dot_general(wgt_ref[ksl, :], glr, (((1,), (1,)), ((), ())),
                                          preferred_element_type=F32) + bgt_ref[ksl, :])
        decay_col = jnp.exp(jnp.sum(jnp.where(real_tok, la_t, 0.0), axis=-1, keepdims=True))
        qc = q_ref[0, :, ksl]
        kc = k_ref[0, :, ksl]
        vc = v_ref[0, :, vsl]
        st = s_ref[0, h]
        qd = (qc * jnp.exp(bc)).astype(BF16)
        o = jnp.dot(qd, st.astype(BF16), preferred_element_type=F32)
        for s in range(n_tok):
            w = jnp.where(row_id >= s, qc * kc[s:s + 1, :] * jnp.exp(bc - bc[s:s + 1, :]), 0.0)
            o = o + jnp.sum(w, axis=-1, keepdims=True) * vc[s:s + 1, :]
        ms = jnp.mean(o * o, axis=-1, keepdims=True)
        zc = z_ref[0, :, vsl]
        o_ref[0, :, vsl] = o * lax.rsqrt(ms + NORM_EPS) * gn * (zc * _sigmoid(zc))
        kd = (kc * jnp.exp(b_end - bc)).astype(BF16)
        upd = lax.dot_general(kd, vc.astype(BF16), (((0,), (0,)), ((), ())),
                              preferred_element_type=F32)
        so_ref[0, h] = st * decay_col + upd


def _gla_sample(ua, glr, wg, bg, gn, state, n_tok):
    b, rows, _ = ua.shape
    h, dk, dv = state.shape[1:]
    k_blk = h * dk // (h * dk)
    v_blk = 2 * h * dk // (h * dv)
    z_blk = v_blk + 1
    return pl.pallas_call(
        functools.partial(_gla_sample_kernel, n_tok=n_tok),
        out_shape=(jax.ShapeDtypeStruct((b, rows, h * dv), F32),
                   jax.ShapeDtypeStruct(state.shape, F32)),
        grid=(b,),
        in_specs=[pl.BlockSpec((1, rows, h * dk), lambda i: (i, 0, 0)),
                  pl.BlockSpec((1, rows, h * dk), lambda i: (i, 0, k_blk)),
                  pl.BlockSpec((1, rows, h * dv), lambda i: (i, 0, v_blk)),
                  pl.BlockSpec((1, rows, h * dv), lambda i: (i, 0, z_blk)),
                  pl.BlockSpec((1, rows, GLA_RANK), lambda i: (i, 0, 0)),
                  pl.BlockSpec((GLA_RANK, h * dk), lambda i: (0, 0)),
                  pl.BlockSpec((1, h * dk), lambda i: (0, 0)),
                  pl.BlockSpec((h * dk, GLA_RANK), lambda i: (0, 0)),
                  pl.BlockSpec((h * dk, 1), lambda i: (0, 0)),
                  pl.BlockSpec((1, dv), lambda i: (0, 0)),
                  pl.BlockSpec((1, h, dk, dv), lambda i: (i, 0, 0, 0))],
        out_specs=(pl.BlockSpec((1, rows, h * dv), lambda i: (i, 0, 0)),
                   pl.BlockSpec((1, h, dk, dv), lambda i: (i, 0, 0, 0))),
        compiler_params=_cparams(("arbitrary",)),
        name="gla_sample",
    )(ua, ua, ua, ua, glr, wg, bg.reshape(1, -1), wg.T, bg.reshape(-1, 1), gn.reshape(1, -1),
      state)


SB_BLK = 2048
SB_SUB = 128


def _suffix_matrix():
    j = np.arange(SB_SUB)
    u = (j[:, None] >= j[None, :]).astype(np.float32)
    half = np.concatenate([u, np.ones((SB_SUB, SB_SUB), np.float32)], axis=1)
    return jnp.asarray(np.concatenate([half, half], axis=0), dtype=BF16)


def _sb_suffix(z, uw, mask):
    neg_abs = lax.bitcast_convert_type(
        lax.bitcast_convert_type(z, jnp.int32) | jnp.int32(-2 ** 31), F32)
    sp = jnp.maximum(z, 0.0) + jnp.log(1.0 + jnp.exp2(neg_abs)) * LOG2E
    if mask is not None:
        sp = jnp.where(mask, sp, 0.0)
    hi = sp.astype(BF16)
    lo = (sp - hi.astype(F32)).astype(BF16)
    return jnp.dot(jnp.concatenate([hi, lo], axis=1), uw, preferred_element_type=F32)


def _sb_weights(z, s2, car, mask):
    a = jnp.exp2(z - s2[:, :SB_SUB] - car)
    if mask is not None:
        a = jnp.where(mask, a, 0.0)
    return a, car + s2[:, SB_SUB:]


def _sb_prompt_kernel(qi_ref, kj_ref, q_ref, k_ref, v_ref, z_ref, bias_ref, uw_ref,
                      o_ref, acc_ref, car_ref):
    step = pl.program_id(1)
    qi = qi_ref[step]
    kj = kj_ref[step]
    blk = q_ref.shape[0]
    nsub = blk // SB_SUB
    uw = uw_ref[...]
    bias = bias_ref[0]

    @pl.when(kj == qi)
    def _():
        acc_ref[...] = jnp.zeros_like(acc_ref)
        car_ref[...] = jnp.zeros_like(car_ref)

    def logits(ks, r0):
        return lax.dot_general(q_ref[r0:blk, :], k_ref[ks * SB_SUB:(ks + 1) * SB_SUB, :],
                               (((1,), (1,)), ((), ())), preferred_element_type=F32) + bias

    def causal_mask(shape, masked):
        if not masked:
            return None
        return lax.broadcasted_iota(jnp.int32, shape, 1) < lax.broadcasted_iota(jnp.int32, shape, 0)

    def finish(ks, r0, z, s2, mask):
        a, car = _sb_weights(z, s2, car_ref[r0:blk, :], mask)
        car_ref[r0:blk, :] = car
        acc_ref[r0:blk, :] += jnp.dot(a.astype(BF16), v_ref[ks * SB_SUB:(ks + 1) * SB_SUB, :],
                                      preferred_element_type=F32)

    def sweep(masked):
        tiles = [(ks, ks * SB_SUB if masked else 0) for ks in range(nsub - 1, -1, -1)]
        z = logits(*tiles[0])
        mask = causal_mask(z.shape, masked)
        s2 = _sb_suffix(z, uw, mask)
        for i, (ks, r0) in enumerate(tiles):
            nxt = tiles[i + 1] if i + 1 < len(tiles) else None
            if nxt is not None:
                z_n = logits(*nxt)
                mask_n = causal_mask(z_n.shape, masked)
            finish(ks, r0, z, s2, mask)
            if nxt is not None:
                s2 = _sb_suffix(z_n, uw, mask_n)
                z, mask = z_n, mask_n

    @pl.when(kj == qi)
    def _():
        sweep(True)

    @pl.when(kj != qi)
    def _():
        sweep(False)

    @pl.when(kj == 0)
    def _():
        zz = z_ref[...]
        o_ref[...] = (acc_ref[...] * (zz * _sigmoid(zz))).astype(o_ref.dtype)


def _sb_prompt(q16, k16, v16, uc, bias):
    t_len, w = q16.shape
    heads = w // SB_HEAD_DIM
    blk = min(SB_BLK, t_len)
    nq = t_len // blk
    qi = np.concatenate([np.full(i + 1, i) for i in range(nq)]).astype(np.int32)
    kj = np.concatenate([np.arange(i, -1, -1) for i in range(nq)]).astype(np.int32)
    bias_b = jnp.broadcast_to((bias.astype(F32) * LOG2E)[:, None, None], (heads, 1, SB_HEAD_DIM))
    qmap = lambda h, s, qi_r, kj_r: (qi_r[s], h)
    kmap = lambda h, s, qi_r, kj_r: (kj_r[s], h)
    return pl.pallas_call(
        _sb_prompt_kernel,
        out_shape=jax.ShapeDtypeStruct((t_len, w), BF16),
        grid_spec=pltpu.PrefetchScalarGridSpec(
            num_scalar_prefetch=2,
            grid=(heads, len(qi)),
            in_specs=[pl.BlockSpec((blk, SB_HEAD_DIM), qmap),
                      pl.BlockSpec((blk, SB_HEAD_DIM), kmap),
                      pl.BlockSpec((blk, SB_HEAD_DIM), kmap),
                      pl.BlockSpec((blk, SB_HEAD_DIM), qmap),
                      pl.BlockSpec((1, 1, SB_HEAD_DIM), lambda h, s, qi_r, kj_r: (h, 0, 0)),
                      pl.BlockSpec((2 * SB_SUB, 2 * SB_SUB), lambda h, s, qi_r, kj_r: (0, 0))],
            out_specs=pl.BlockSpec((blk, SB_HEAD_DIM), qmap),
            scratch_shapes=[pltpu.VMEM((blk, SB_HEAD_DIM), F32),
                            pltpu.VMEM((blk, SB_HEAD_DIM), F32)]),
        compiler_params=_cparams(("arbitrary", "arbitrary")),
        name="sb_prompt",
    )(jnp.asarray(qi), jnp.asarray(kj), q16, k16, v16, uc, bias_b, _suffix_matrix())


SB_PAIR = 2
SB_SEQS = 4


def _sb_sample_kernel(pt_ref, q_ref, kc_ref, vc_ref, kn_ref, vn_ref, z_ref, bias_ref, uw_ref,
                      o_ref, kbuf, vbuf, sem, acc_ref, car_ref, kpg_ref, vpg_ref, *, n_tok, n_pages):
    n_seq, heads, rows, _ = q_ref.shape
    nsq = kbuf.shape[1]
    page_len = kbuf.shape[3]
    ngrp = heads // SB_PAIR
    total = (n_seq // nsq) * n_pages
    uw = uw_ref[...]
    bias = bias_ref[...]

    def page_copies(i, slot):
        i = jnp.minimum(i, total - 1)
        b0 = (i // n_pages) * nsq
        p = i % n_pages
        out = []
        for s in range(nsq):
            pg = pt_ref[(b0 + s) * n_pages + (n_pages - 1 - p)]
            for h in range(heads):
                out.append(pltpu.make_async_copy(kc_ref.at[pg, :, h, :], kbuf.at[slot, s, h],
                                                 sem.at[0, slot]))
                out.append(pltpu.make_async_copy(vc_ref.at[pg, :, h, :], vbuf.at[slot, s, h],
                                                 sem.at[1, slot]))
        return out

    def pages(b0, k_head, v_head, masked):
        zs = []
        for s in range(nsq):
            for g in range(ngrp):
                zg = None
                for h in range(g * SB_PAIR, (g + 1) * SB_PAIR):
                    zh = lax.dot_general(q_ref[b0 + s, h], k_head(s, h), (((1,), (1,)), ((), ())),
                                         preferred_element_type=F32)
                    zg = zh if zg is None else zg + zh
                zs.append(zg)
        z = jnp.concatenate(zs, axis=0) + bias
        mask = None
        if masked:
            tok = lax.broadcasted_iota(jnp.int32, z.shape, 0) % n_tok
            mask = lax.broadcasted_iota(jnp.int32, z.shape, 1) < tok
        a, car = _sb_weights(z, _sb_suffix(z, uw, mask), car_ref[...], mask)
        car_ref[...] = car
        a16 = a.astype(BF16)
        for s in range(nsq):
            for h in range(heads):
                r0 = (s * ngrp + h // SB_PAIR) * rows
                acc_ref[s, h] += jnp.dot(a16[r0:r0 + rows, :], v_head(s, h),
                                         preferred_element_type=F32)

    kpg_ref[...] = jnp.zeros_like(kpg_ref)
    vpg_ref[...] = jnp.zeros_like(vpg_ref)
    for c in page_copies(0, 0):
        c.start()

    def step(i, carry):
        slot = i % 2
        b0 = (i // n_pages) * nsq
        p = i % n_pages
        for c in page_copies(i + 1, 1 - slot):
            c.start()

        @pl.when(p == 0)
        def _():
            acc_ref[...] = jnp.zeros_like(acc_ref)
            car_ref[...] = jnp.zeros_like(car_ref)
            for s in range(nsq):
                kpg_ref[s, 0:kn_ref.shape[1], :] = kn_ref[b0 + s]
                vpg_ref[s, 0:vn_ref.shape[1], :] = vn_ref[b0 + s]
            pages(b0,
                  lambda s, h: kpg_ref[s, pl.ds(h, page_len, stride=heads), :].astype(BF16),
                  lambda s, h: vpg_ref[s, pl.ds(h, page_len, stride=heads), :].astype(BF16), True)

        for c in page_copies(i, slot):
            c.wait()
        pages(b0, lambda s, h: kbuf[slot, s, h].astype(BF16),
              lambda s, h: vbuf[slot, s, h].astype(BF16), False)

        @pl.when(p == n_pages - 1)
        def _():
            first = lax.broadcasted_iota(jnp.int32, (rows, SB_HEAD_DIM), 0) < n_tok
            for s in range(nsq):
                for g in range(ngrp):
                    og = jnp.where(first, acc_ref[s, g * SB_PAIR], acc_ref[s, g * SB_PAIR + 1])
                    zz = z_ref[b0 + s, g]
                    o_ref[b0 + s, g] = og * (zz * _sigmoid(zz))
        return carry

    lax.fori_loop(0, total, step, 0)
    for c in page_copies(total, total % 2):
        c.wait()


def _sb_sample(page_table, q8, cache_k, cache_v, k_new, v_new, z_g, bias_rows, n_tok):
    b, heads, rows, _ = q8.shape
    ngrp = heads // SB_PAIR
    n_pages = page_table.shape[1]
    page = cache_k.shape[1]
    nsq = SB_SEQS
    assert b % nsq == 0
    vmem = pl.BlockSpec(memory_space=pltpu.VMEM)
    hbm = pl.BlockSpec(memory_space=pl.ANY)
    return pl.pallas_call(
        functools.partial(_sb_sample_kernel, n_tok=n_tok, n_pages=n_pages),
        out_shape=jax.ShapeDtypeStruct((b, ngrp, rows, SB_HEAD_DIM), F32),
        in_specs=[pl.BlockSpec(memory_space=pltpu.SMEM), vmem, hbm, hbm,
                  vmem, vmem, vmem, vmem, vmem],
        out_specs=vmem,
        scratch_shapes=[pltpu.VMEM((2, nsq, heads, page, SB_HEAD_DIM), F32),
                        pltpu.VMEM((2, nsq, heads, page, SB_HEAD_DIM), F32),
                        pltpu.SemaphoreType.DMA((2, 2)),
                        pltpu.VMEM((nsq, heads, rows, SB_HEAD_DIM), F32),
                        pltpu.VMEM((nsq * ngrp * rows, SB_HEAD_DIM), F32),
                        pltpu.VMEM((nsq, page * heads, SB_HEAD_DIM), F32),
                        pltpu.VMEM((nsq, page * heads, SB_HEAD_DIM), F32)],
        compiler_params=pltpu.CompilerParams(vmem_limit_bytes=VMEM_LIMIT),
        name="sb_sample",
    )(page_table.reshape(-1), q8, cache_k, cache_v, k_new, v_new, z_g,
      jnp.tile(bias_rows, (nsq, 1)), _suffix_matrix())


def _pad_rows(a, rows):
    return jnp.pad(a, ((0, 0), (0, rows - a.shape[1]), (0, 0)))


def _layer(xp, xs, c_all, w, cache_k, cache_v, state, page_table, n_seq, n_tok):
    (w_ada, b_ada, norm_g, w_in, w_gate_up, b_gate, gla_norm_g,
     q_norm_g, k_norm_g, sb_bias, w_branch_a, w_branch_b, w_out) = w
    d = xp.shape[1]
    heads = d // SB_HEAD_DIM
    dv = gla_norm_g.shape[0]
    key_w = w_gate_up.shape[1]
    dk = key_w // GLA_HEADS
    val_w = GLA_HEADS * dv
    sb_w = heads * SB_HEAD_DIM

    c0 = 2 * key_w + 2 * val_w
    c1 = c0 + GLA_RANK
    w_a = w_in[:, :c0].astype(BF16)
    w_glr = w_in[:, c0:c1].astype(BF16)
    w_b = w_in[:, c1:].astype(BF16)
    wa16, wb16, wo16 = (w_branch_a.astype(BF16), w_branch_b.astype(BF16), w_out.astype(BF16))
    wg16 = w_gate_up.astype(BF16)

    mod = _ada(c_all, w_ada, b_ada)
    shift, scale, gate = mod[:, :d], mod[:, d:2 * d], mod[:, 2 * d:]
    rep = lambda a: jnp.repeat(a[:n_seq], n_tok, axis=0)
    q_scale = LOG2E / math.sqrt(SB_HEAD_DIM)

    def in_proj(h, tm, want_16):
        kv_types = (F32, BF16) if want_16 else (F32,)
        (ua,) = _proj(h, w_a, 0, c0, (F32,), tm, 1024, "proj_gla",
                      lead_cols=key_w, lead_scale=dk ** -0.5)
        (glr,) = _proj(h, w_glr, 0, GLA_RANK, (F32,), tm, GLA_RANK, "proj_glr")
        (q16,) = _proj(h, w_b, 0, sb_w, (BF16,), tm, 1024, "proj_qb",
                       norm_gain=q_norm_g, cast_scale=q_scale)
        k_out = _proj(h, w_b, sb_w, sb_w, kv_types, tm, 1024, "proj_kb", norm_gain=k_norm_g)
        v_out = _proj(h, w_b, 2 * sb_w, sb_w, kv_types, tm, 1024, "proj_vb")
        (uc,) = _proj(h, w_b, 3 * sb_w, sb_w + 2 * d, (F32,), tm, 1024, "proj_gates")
        return ua, glr, q16, k_out, v_out, uc

    hp = _mod_norm(xp, norm_g, scale[n_seq:n_seq + 1], shift[n_seq:n_seq + 1], 512)
    ua, glr, q16, (k32, k16), (v32, v16), uc = in_proj(hp, 1024, True)
    oa, s_p = _gla_prompt(ua, glr, wg16, b_gate, gla_norm_g, dk, dv)
    ob = _sb_prompt(q16, k16, v16, uc, sb_bias)
    y = _merge(oa, ob, wa16, wb16, uc, sb_w, sb_w + d, 512, 1024)
    out_p = _out_proj(y, wo16, xp, gate[n_seq:n_seq + 1], 1024, 1024)
    kp, vp = k32, v32

    m_s = xs.shape[0]
    hs = _mod_norm(xs, norm_g, rep(scale), rep(shift), m_s)
    ua, glr, q16, (k32,), (v32,), uc = in_proj(hs, m_s, False)
    ua3 = _pad_rows(ua.reshape(n_seq, n_tok, -1), GLA_PAD_ROWS)
    glr3 = _pad_rows(glr.reshape(n_seq, n_tok, -1), GLA_PAD_ROWS)
    oa3, s_s = _gla_sample(ua3, glr3, wg16, b_gate, gla_norm_g, state, n_tok)
    oa = oa3[:, :n_tok].reshape(m_s, val_w).astype(BF16)

    ngrp = heads // SB_PAIR
    qg = q16.reshape(n_seq, n_tok, ngrp, SB_PAIR, SB_HEAD_DIM).transpose(0, 2, 3, 1, 4)
    eye = jnp.eye(SB_PAIR, dtype=BF16)[None, None, :, :, None, None]
    q8 = (qg[:, :, :, None, :, :] * eye).reshape(n_seq, heads, SB_PAIR * n_tok, SB_HEAD_DIM)
    z_g = (uc[:, :sb_w].reshape(n_seq, n_tok, ngrp, SB_PAIR, SB_HEAD_DIM)
           .transpose(0, 2, 3, 1, 4).reshape(n_seq, ngrp, SB_PAIR * n_tok, SB_HEAD_DIM))
    bias_rows = jnp.broadcast_to(jnp.repeat(sb_bias.astype(F32) * LOG2E, n_tok)[:, None],
                                 (heads * n_tok, SB_HEAD_DIM))
    k_new = k32.reshape(n_seq, n_tok * heads, SB_HEAD_DIM)
    v_new = v32.reshape(n_seq, n_tok * heads, SB_HEAD_DIM)
    ob_g = _sb_sample(page_table, q8, cache_k, cache_v, k_new, v_new, z_g, bias_rows, n_tok)
    ob = (ob_g.reshape(n_seq, ngrp, SB_PAIR, n_tok, SB_HEAD_DIM).transpose(0, 3, 1, 2, 4)
          .reshape(m_s, sb_w).astype(BF16))
    y = _merge(oa, ob, wa16, wb16, uc, sb_w, sb_w + d, m_s, 1024)
    out_s = _out_proj(y, wo16, xs, rep(gate), m_s, 1024)
    return out_p, out_s, kp, vp, s_p, k32, v32, s_s


def kernel(x_prompt, x_sample, cache_k, cache_v, state_gla, page_table, c_prompt, c_sample,
           w_ada, b_ada, norm_g, w_in, w_gate_up, b_gate, gla_norm_g, q_norm_g, k_norm_g,
           sb_bias, w_branch_a, w_branch_b, w_out):
    depth = w_ada.shape[0]
    n_b, t_len, d = x_prompt.shape
    assert n_b == 1, "the prompt group is a single sequence"
    n_seq, n_tok, _ = x_sample.shape
    heads = d // SB_HEAD_DIM
    pad = (-(n_seq + n_b)) % 8
    c_all = jnp.concatenate([c_sample, c_prompt, jnp.zeros((pad, d), F32)], axis=0)
    xp = x_prompt.reshape(t_len, d)
    xs = x_sample.reshape(n_seq * n_tok, d)
    outs = [[] for _ in range(6)]
    for l in range(depth):
        w = (w_ada[l], b_ada[l], norm_g[l], w_in[l], w_gate_up[l], b_gate[l], gla_norm_g[l],
             q_norm_g[l], k_norm_g[l], sb_bias[l], w_branch_a[l], w_branch_b[l], w_out[l])
        xp, xs, kp, vp, sp, ks, vs, ss = _layer(xp, xs, c_all, w, cache_k[l], cache_v[l],
                                                 state_gla[l], page_table, n_seq, n_tok)
        outs[0].append(kp.reshape(n_b, t_len, heads, SB_HEAD_DIM))
        outs[1].append(vp.reshape(n_b, t_len, heads, SB_HEAD_DIM))
        outs[2].append(sp[None])
        outs[3].append(ks.reshape(n_seq, n_tok, heads, SB_HEAD_DIM))
        outs[4].append(vs.reshape(n_seq, n_tok, heads, SB_HEAD_DIM))
        outs[5].append(ss)
    k_p, v_p, s_p, k_s, v_s, s_s = (jnp.stack(o) for o in outs)
    return (xp.reshape(n_b, t_len, d), xs.reshape(n_seq, n_tok, d), k_p, v_p, s_p, k_s, v_s, s_s)
```

```python
import functools
import math

import numpy as np
import jax
import jax.numpy as jnp
from jax import lax
from jax.experimental import pallas as pl
from jax.experimental.pallas import tpu as pltpu

F32 = jnp.float32
BF16 = jnp.bfloat16

GLA_HEADS = 4
GLA_RANK = 16
GLA_TAU = 16.0
SB_HEAD_DIM = 128
NORM_EPS = 1e-6
LOG2E = 1.4426950408889634
LN2 = 0.6931471805599453

LANES = 128
VMEM_LIMIT = 56 * 1024 * 1024


def _cparams(sem):
    return pltpu.CompilerParams(dimension_semantics=sem, vmem_limit_bytes=VMEM_LIMIT)


def _sigmoid(x):
    return 1.0 / (1.0 + jnp.exp(-x))


def _softplus_parts(z):
    l = jnp.log(1.0 + jnp.exp(-jnp.abs(z)))
    return jnp.maximum(z, 0.0) + l, l


def _ada_kernel(c_ref, w_ref, b_ref, o_ref):
    c = c_ref[...]
    s = (c * _sigmoid(c)).astype(BF16)
    o_ref[...] = jnp.dot(s, w_ref[...].astype(BF16), preferred_element_type=F32) + b_ref[...]


def _ada(c_all, w_ada, b_ada, tn=512):
    r, d = c_all.shape
    n = w_ada.shape[1]
    return pl.pallas_call(
        _ada_kernel,
        out_shape=jax.ShapeDtypeStruct((r, n), F32),
        grid=(n // tn,),
        in_specs=[pl.BlockSpec((r, d), lambda j: (0, 0)),
                  pl.BlockSpec((d, tn), lambda j: (0, j)),
                  pl.BlockSpec((1, tn), lambda j: (0, j))],
        out_specs=pl.BlockSpec((r, tn), lambda j: (0, j)),
        compiler_params=_cparams(("arbitrary",)),
        name="ada_mod",
    )(c_all, w_ada, b_ada.reshape(1, n))


def _norm_kernel(x_ref, g_ref, scale_ref, shift_ref, h_ref):
    x = x_ref[...]
    ms = jnp.mean(x * x, axis=-1, keepdims=True)
    y = x * lax.rsqrt(ms + NORM_EPS) * g_ref[...]
    h_ref[...] = (y * (1.0 + scale_ref[...]) + shift_ref[...]).astype(h_ref.dtype)


def _mod_norm(x, g, scale, shift, tm):
    m, d = x.shape
    per_row = scale.shape[0] != 1
    mod_spec = (pl.BlockSpec((tm, d), lambda i: (i, 0)) if per_row
                else pl.BlockSpec((1, d), lambda i: (0, 0)))
    return pl.pallas_call(
        _norm_kernel,
        out_shape=jax.ShapeDtypeStruct((m, d), BF16),
        grid=(m // tm,),
        in_specs=[pl.BlockSpec((tm, d), lambda i: (i, 0)),
                  pl.BlockSpec((1, d), lambda i: (0, 0)),
                  mod_spec, mod_spec],
        out_specs=pl.BlockSpec((tm, d), lambda i: (i, 0)),
        compiler_params=_cparams(("arbitrary",)),
        name="mod_norm",
    )(x, g.reshape(1, d), scale, shift)


def _proj_kernel(h_ref, w_ref, *refs, lead_tiles, lead_scale, head_norm, cast_scale):
    o_refs = refs[1:] if head_norm else refs
    acc = jnp.dot(h_ref[...], w_ref[...], preferred_element_type=F32)
    if lead_tiles:
        acc = acc * jnp.where(pl.program_id(1) < lead_tiles, lead_scale, 1.0)

    def emit(cols, y):
        for o_ref in o_refs:
            o_ref[:, cols] = y if o_ref.dtype == F32 else (y * cast_scale).astype(o_ref.dtype)

    if head_norm:
        g = refs[0][...]
        for c in range(acc.shape[1] // SB_HEAD_DIM):
            cols = slice(c * SB_HEAD_DIM, (c + 1) * SB_HEAD_DIM)
            blk = acc[:, cols]
            ms = jnp.mean(blk * blk, axis=-1, keepdims=True)
            emit(cols, blk * lax.rsqrt(ms + NORM_EPS) * g)
    else:
        emit(slice(None), acc)


def _proj(h, w, col0, n, out_dtypes, tm, tn, name, *, lead_cols=0, lead_scale=1.0,
          norm_gain=None, cast_scale=1.0):
    m, k = h.shape
    tn = min(tn, n)
    j0 = col0 // tn
    assert col0 % tn == 0 and n % tn == 0 and m % tm == 0 and lead_cols % tn == 0
    in_specs = [pl.BlockSpec((tm, k), lambda i, j: (i, 0)),
                pl.BlockSpec((k, tn), lambda i, j: (0, j0 + j))]
    args = [h, w]
    if norm_gain is not None:
        in_specs.append(pl.BlockSpec((1, SB_HEAD_DIM), lambda i, j: (0, 0)))
        args.append(norm_gain.reshape(1, SB_HEAD_DIM))
    tile = pl.BlockSpec((tm, tn), lambda i, j: (i, j))
    return pl.pallas_call(
        functools.partial(_proj_kernel, lead_tiles=lead_cols // tn, lead_scale=lead_scale,
                          head_norm=norm_gain is not None, cast_scale=cast_scale),
        out_shape=tuple(jax.ShapeDtypeStruct((m, n), dt) for dt in out_dtypes),
        grid=(m // tm, n // tn),
        in_specs=in_specs,
        out_specs=tuple(tile for _ in out_dtypes),
        compiler_params=_cparams(("arbitrary", "arbitrary")),
        name=name,
    )(*args)


def _merge_kernel(oa_ref, ob_ref, wa_ref, wb_ref, ga_ref, gb_ref, y_ref):
    ya = jnp.dot(oa_ref[...], wa_ref[...], preferred_element_type=F32)
    yb = jnp.dot(ob_ref[...], wb_ref[...], preferred_element_type=F32)
    y_ref[...] = (_sigmoid(ga_ref[...]) * ya + _sigmoid(gb_ref[...]) * yb).astype(y_ref.dtype)


def _merge(oa, ob, wa, wb, uc, ga_col, gb_col, tm, tn):
    m, k = oa.shape
    n = wa.shape[1]
    row = lambda i, j: (i, 0)
    col = lambda i, j: (0, j)
    ja, jb = ga_col // tn, gb_col // tn
    return pl.pallas_call(
        _merge_kernel,
        out_shape=jax.ShapeDtypeStruct((m, n), BF16),
        grid=(m // tm, n // tn),
        in_specs=[pl.BlockSpec((tm, k), row), pl.BlockSpec((tm, k), row),
                  pl.BlockSpec((k, tn), col), pl.BlockSpec((k, tn), col),
                  pl.BlockSpec((tm, tn), lambda i, j: (i, ja + j)),
                  pl.BlockSpec((tm, tn), lambda i, j: (i, jb + j))],
        out_specs=pl.BlockSpec((tm, tn), lambda i, j: (i, j)),
        compiler_params=_cparams(("arbitrary", "arbitrary")),
        name="merge",
    )(oa, ob, wa, wb, uc, uc)


def _out_kernel(y_ref, w_ref, x_ref, gate_ref, o_ref):
    acc = jnp.dot(y_ref[...], w_ref[...], preferred_element_type=F32)
    o_ref[...] = x_ref[...] + gate_ref[...] * acc


def _out_proj(y, w, x, gate, tm, tn):
    m, k = y.shape
    n = w.shape[1]
    per_row = gate.shape[0] != 1
    gate_spec = (pl.BlockSpec((tm, tn), lambda i, j: (i, j)) if per_row
                 else pl.BlockSpec((1, tn), lambda i, j: (0, j)))
    return pl.pallas_call(
        _out_kernel,
        out_shape=jax.ShapeDtypeStruct((m, n), F32),
        grid=(m // tm, n // tn),
        in_specs=[pl.BlockSpec((tm, k), lambda i, j: (i, 0)),
                  pl.BlockSpec((k, tn), lambda i, j: (0, j)),
                  pl.BlockSpec((tm, tn), lambda i, j: (i, j)),
                  gate_spec],
        out_specs=pl.BlockSpec((tm, tn), lambda i, j: (i, j)),
        compiler_params=_cparams(("arbitrary", "arbitrary")),
        name="out_proj",
    )(y, w, x, gate)


GLA_SUB = 16
GLA_BLK = 64
GLA_ROWS = 256


def _split3(x):
    hi = x.astype(BF16)
    r1 = x - hi.astype(F32)
    mid = r1.astype(BF16)
    lo = (r1 - mid.astype(F32)).astype(BF16)
    return hi, mid, lo


def _gla_prompt_kernel(q_ref, k_ref, v_ref, z_ref, glr_ref, wg_ref, bg_ref, gn_ref, tri_ref,
                       o_ref, s_ref, st_ref, b_ref):
    t = pl.program_id(0)
    nt = pl.num_programs(0)
    rows = q_ref.shape[0]
    nh, dk, dv = st_ref.shape
    c = GLA_SUB
    blk = min(GLA_BLK, rows)

    @pl.when(t == 0)
    def _():
        st_ref[...] = jnp.zeros_like(st_ref)

    x = jnp.dot(glr_ref[...].astype(BF16), wg_ref[...], preferred_element_type=F32) + bg_ref[...]
    la = (jnp.minimum(x, 0.0) - jnp.log(1.0 + jnp.exp(-jnp.abs(x)))) * (1.0 / GLA_TAU)
    tri = tri_ref[...]
    hi, mid, lo = _split3(la)
    b_ref[...] = (jnp.dot(tri, hi, preferred_element_type=F32)
                  + jnp.dot(tri, mid, preferred_element_type=F32)
                  + jnp.dot(tri, lo, preferred_element_type=F32))

    row_id = lax.broadcasted_iota(jnp.int32, (c // 2, 1), 0)
    lane_id = lax.broadcasted_iota(jnp.int32, (c // 2, c), 1)
    gn = gn_ref[...]

    heads = range(nh)
    ksl = [slice(h * dk, (h + 1) * dk) for h in heads]
    vsl = [slice(h * dv, (h + 1) * dv) for h in heads]
    nt_dims = (((1,), (1,)), ((), ()))

    def sub_chunk(r0, i, st16):
        ri = r0 + i * c
        rs = pl.ds(ri, c)
        qc = [q_ref[rs, ksl[h]] for h in heads]
        bc = [b_ref[rs, ksl[h]] for h in heads]
        o_acc = [jnp.dot((qc[h] * jnp.exp(bc[h])).astype(BF16), st16[h],
                         preferred_element_type=F32) for h in heads]
        if i > 0:
            prev = pl.ds(r0, i * c)
            for h in heads:
                b_ref0 = b_ref[pl.ds(ri - 1, 1), ksl[h]]
                qq = (qc[h] * jnp.exp(bc[h] - b_ref0)).astype(BF16)
                kk = (k_ref[prev, ksl[h]] * jnp.exp(b_ref0 - b_ref[prev, ksl[h]])).astype(BF16)
                a_off = lax.dot_general(qq, kk, nt_dims, preferred_element_type=F32)
                o_acc[h] = o_acc[h] + jnp.dot(a_off.astype(BF16), v_ref[prev, vsl[h]].astype(BF16),
                                              preferred_element_type=F32)
        half = c // 2
        a_top = [jnp.zeros((half, c), F32) for _ in heads]
        a_bot = [jnp.zeros((half, c), F32) for _ in heads]
        for s in range(c):
            for h in heads:
                ks = k_ref[pl.ds(ri + s, 1), ksl[h]]
                bs = b_ref[pl.ds(ri + s, 1), ksl[h]]
                if s < half:
                    w = jnp.where(row_id >= s,
                                  qc[h][:half] * ks * jnp.exp(bc[h][:half] - bs), 0.0)
                    a_top[h] = jnp.where(lane_id == s, jnp.sum(w, axis=-1, keepdims=True), a_top[h])
                    w = qc[h][half:] * ks * jnp.exp(bc[h][half:] - bs)
                else:
                    w = jnp.where(row_id >= s - half,
                                  qc[h][half:] * ks * jnp.exp(bc[h][half:] - bs), 0.0)
                a_bot[h] = jnp.where(lane_id == s, jnp.sum(w, axis=-1, keepdims=True), a_bot[h])
        for h in heads:
            a = jnp.concatenate([a_top[h], a_bot[h]], axis=0).astype(BF16)
            o = o_acc[h] + jnp.dot(a, v_ref[rs, vsl[h]].astype(BF16), preferred_element_type=F32)
            ms = jnp.mean(o * o, axis=-1, keepdims=True)
            zc = z_ref[rs, vsl[h]]
            y = o * lax.rsqrt(ms + NORM_EPS) * gn * (zc * _sigmoid(zc))
            o_ref[rs, vsl[h]] = y.astype(o_ref.dtype)

    def block(bi, carry):
        r0 = pl.multiple_of(bi * blk, blk)
        rb = pl.ds(r0, blk)
        st = [st_ref[h] for h in heads]
        st16 = [s.astype(BF16) for s in st]
        for i in range(blk // c):
            sub_chunk(r0, i, st16)
        for h in heads:
            bb = b_ref[rb, ksl[h]]
            kd = (k_ref[rb, ksl[h]] * jnp.exp(bb[blk - 1:blk] - bb)).astype(BF16)
            upd = lax.dot_general(kd, v_ref[rb, vsl[h]].astype(BF16), (((0,), (0,)), ((), ())),
                                  preferred_element_type=F32)
            st_ref[h] = st[h] * jnp.exp(bb[blk - 8:].T[:, 7:8]) + upd
        return carry

    lax.fori_loop(0, rows // blk, block, 0)

    @pl.when(t == nt - 1)
    def _():
        s_ref[...] = st_ref[...]


def _gla_prompt(ua, glr, wg, bg, gn, dk, dv):
    t_len = ua.shape[0]
    h = GLA_HEADS
    rows = min(GLA_ROWS, t_len)
    r = np.arange(rows)
    tri = ((r[:, None] >= r[None, :]) & (r[:, None] // GLA_BLK == r[None, :] // GLA_BLK))
    tri = jnp.asarray(tri, dtype=BF16)
    v_blk = 2 * h * dk // (h * dv)
    return pl.pallas_call(
        _gla_prompt_kernel,
        out_shape=(jax.ShapeDtypeStruct((t_len, h * dv), BF16),
                   jax.ShapeDtypeStruct((h, dk, dv), F32)),
        grid=(t_len // rows,),
        in_specs=[pl.BlockSpec((rows, h * dk), lambda tt: (tt, 0)),
                  pl.BlockSpec((rows, h * dk), lambda tt: (tt, 1)),
                  pl.BlockSpec((rows, h * dv), lambda tt: (tt, v_blk)),
                  pl.BlockSpec((rows, h * dv), lambda tt: (tt, v_blk + 1)),
                  pl.BlockSpec((rows, GLA_RANK), lambda tt: (tt, 0)),
                  pl.BlockSpec((GLA_RANK, h * dk), lambda tt: (0, 0)),
                  pl.BlockSpec((1, h * dk), lambda tt: (0, 0)),
                  pl.BlockSpec((1, dv), lambda tt: (0, 0)),
                  pl.BlockSpec((rows, rows), lambda tt: (0, 0))],
        out_specs=(pl.BlockSpec((rows, h * dv), lambda tt: (tt, 0)),
                   pl.BlockSpec((h, dk, dv), lambda tt: (0, 0, 0))),
        scratch_shapes=[pltpu.VMEM((h, dk, dv), F32), pltpu.VMEM((rows, h * dk), F32)],
        compiler_params=_cparams(("arbitrary",)),
        name="gla_prompt",
    )(ua, ua, ua, ua, glr, wg, bg.reshape(1, -1), gn.reshape(1, -1), tri)


GLA_PAD_ROWS = 8


def _log_decay(x):
    return (jnp.minimum(x, 0.0) - jnp.log(1.0 + jnp.exp(-jnp.abs(x)))) * (1.0 / GLA_TAU)


def _gla_sample_kernel(q_ref, k_ref, v_ref, z_ref, glr_ref, wg_ref, bg_ref, gn_ref, s_ref,
                       o_ref, so_ref, *, n_tok):
    rows = q_ref.shape[1]
    dk = s_ref.shape[2]
    dv = s_ref.shape[3]
    row_id = lax.broadcasted_iota(jnp.int32, (rows, 1), 0)
    gn = gn_ref[...]
    glr = glr_ref[0].astype(BF16)
    for h in range(GLA_HEADS):
        ksl = slice(h * dk, (h + 1) * dk)
        vsl = slice(h * dv, (h + 1) * dv)
        la = _log_decay(jnp.dot(glr, wg_ref[:, ksl], preferred_element_type=F32) + bg_ref[:, ksl])
        bc = jnp.zeros_like(la)
        for s in range(n_tok):
            bc = bc + jnp.where(row_id >= s, la[s:s + 1, :], 0.0)
        b_end = bc[n_tok - 1:n_tok, :]
        decay_col = jnp.exp(bc.T[:, n_tok - 1:n_tok])
        qc = q_ref[0, :, ksl]
        kc = k_ref[0, :, ksl]
        vc = v_ref[0, :, vsl]
        st = s_ref[0, h]
        qd = (qc * jnp.exp(bc)).astype(BF16)
        o = jnp.dot(qd, st.astype(BF16), preferred_element_type=F32)
        for s in range(n_tok):
            w = jnp.where(row_id >= s, qc * kc[s:s + 1, :] * jnp.exp(bc - bc[s:s + 1, :]), 0.0)
            o = o + jnp.sum(w, axis=-1, keepdims=True) * vc[s:s + 1, :]
        ms = jnp.mean(o * o, axis=-1, keepdims=True)
        zc = z_ref[0, :, vsl]
        o_ref[0, :, vsl] = o * lax.rsqrt(ms + NORM_EPS) * gn * (zc * _sigmoid(zc))
        kd = (kc * jnp.exp(b_end - bc)).astype(BF16)
        upd = lax.dot_general(kd, vc.astype(BF16), (((0,), (0,)), ((), ())),
                              preferred_element_type=F32)
        so_ref[0, h] = st * decay_col + upd


def _gla_sample(ua, glr, wg, bg, gn, state, n_tok):
    b, rows, _ = ua.shape
    h, dk, dv = state.shape[1:]
    k_blk = h * dk // (h * dk)
    v_blk = 2 * h * dk // (h * dv)
    z_blk = v_blk + 1
    return pl.pallas_call(
        functools.partial(_gla_sample_kernel, n_tok=n_tok),
        out_shape=(jax.ShapeDtypeStruct((b, rows, h * dv), F32),
                   jax.ShapeDtypeStruct(state.shape, F32)),
        grid=(b,),
        in_specs=[pl.BlockSpec((1, rows, h * dk), lambda i: (i, 0, 0)),
                  pl.BlockSpec((1, rows, h * dk), lambda i: (i, 0, k_blk)),
                  pl.BlockSpec((1, rows, h * dv), lambda i: (i, 0, v_blk)),
                  pl.BlockSpec((1, rows, h * dv), lambda i: (i, 0, z_blk)),
                  pl.BlockSpec((1, rows, GLA_RANK), lambda i: (i, 0, 0)),
                  pl.BlockSpec((GLA_RANK, h * dk), lambda i: (0, 0)),
                  pl.BlockSpec((1, h * dk), lambda i: (0, 0)),
                  pl.BlockSpec((1, dv), lambda i: (0, 0)),
                  pl.BlockSpec((1, h, dk, dv), lambda i: (i, 0, 0, 0))],
        out_specs=(pl.BlockSpec((1, rows, h * dv), lambda i: (i, 0, 0)),
                   pl.BlockSpec((1, h, dk, dv), lambda i: (i, 0, 0, 0))),
        compiler_params=_cparams(("arbitrary",)),
        name="gla_sample",
    )(ua, ua, ua, ua, glr, wg, bg.reshape(1, -1), gn.reshape(1, -1), state)


SB_BLK = 2048
SB_SUB = 128


def _suffix_matrix():
    j = np.arange(SB_SUB)
    u = (j[:, None] >= j[None, :]).astype(np.float32)
    half = np.concatenate([u, np.ones((SB_SUB, SB_SUB), np.float32)], axis=1)
    return jnp.asarray(np.concatenate([half, half], axis=0), dtype=BF16)


def _sb_suffix(z, uw, mask):
    neg_abs = lax.bitcast_convert_type(
        lax.bitcast_convert_type(z, jnp.int32) | jnp.int32(-2 ** 31), F32)
    sp = jnp.maximum(z, 0.0) + jnp.log(1.0 + jnp.exp2(neg_abs)) * LOG2E
    if mask is not None:
        sp = jnp.where(mask, sp, 0.0)
    hi = sp.astype(BF16)
    lo = (sp - hi.astype(F32)).astype(BF16)
    return jnp.dot(jnp.concatenate([hi, lo], axis=1), uw, preferred_element_type=F32)


def _sb_weights(z, s2, car, mask):
    a = jnp.exp2(z - s2[:, :SB_SUB] - car)
    if mask is not None:
        a = jnp.where(mask, a, 0.0)
    return a, car + s2[:, SB_SUB:]


def _sb_prompt_kernel(qi_ref, kj_ref, q_ref, k_ref, v_ref, z_ref, bias_ref, uw_ref,
                      o_ref, acc_ref, car_ref):
    step = pl.program_id(1)
    qi = qi_ref[step]
    kj = kj_ref[step]
    blk = q_ref.shape[0]
    nsub = blk // SB_SUB
    uw = uw_ref[...]
    bias = bias_ref[0]

    @pl.when(kj == qi)
    def _():
        acc_ref[...] = jnp.zeros_like(acc_ref)
        car_ref[...] = jnp.zeros_like(car_ref)

    def logits(ks, r0):
        return lax.dot_general(q_ref[r0:blk, :], k_ref[ks * SB_SUB:(ks + 1) * SB_SUB, :],
                               (((1,), (1,)), ((), ())), preferred_element_type=F32) + bias

    def causal_mask(shape, masked):
        if not masked:
            return None
        return lax.broadcasted_iota(jnp.int32, shape, 1) < lax.broadcasted_iota(jnp.int32, shape, 0)

    def finish(ks, r0, z, s2, mask):
        a, car = _sb_weights(z, s2, car_ref[r0:blk, :], mask)
        car_ref[r0:blk, :] = car
        acc_ref[r0:blk, :] += jnp.dot(a.astype(BF16), v_ref[ks * SB_SUB:(ks + 1) * SB_SUB, :],
                                      preferred_element_type=F32)

    def sweep(masked):
        tiles = [(ks, ks * SB_SUB if masked else 0) for ks in range(nsub - 1, -1, -1)]
        z = logits(*tiles[0])
        mask = causal_mask(z.shape, masked)
        s2 = _sb_suffix(z, uw, mask)
        for i, (ks, r0) in enumerate(tiles):
            nxt = tiles[i + 1] if i + 1 < len(tiles) else None
            if nxt is not None:
                z_n = logits(*nxt)
                mask_n = causal_mask(z_n.shape, masked)
            finish(ks, r0, z, s2, mask)
            if nxt is not None:
                s2 = _sb_suffix(z_n, uw, mask_n)
                z, mask = z_n, mask_n

    @pl.when(kj == qi)
    def _():
        sweep(True)

    @pl.when(kj != qi)
    def _():
        sweep(False)

    @pl.when(kj == 0)
    def _():
        zz = z_ref[...]
        o_ref[...] = (acc_ref[...] * (zz * _sigmoid(zz))).astype(o_ref.dtype)


def _sb_prompt(q16, k16, v16, uc, bias):
    t_len, w = q16.shape
    heads = w // SB_HEAD_DIM
    blk = min(SB_BLK, t_len)
    nq = t_len // blk
    qi = np.concatenate([np.full(i + 1, i) for i in range(nq)]).astype(np.int32)
    kj = np.concatenate([np.arange(i, -1, -1) for i in range(nq)]).astype(np.int32)
    bias_b = jnp.broadcast_to((bias.astype(F32) * LOG2E)[:, None, None], (heads, 1, SB_HEAD_DIM))
    qmap = lambda h, s, qi_r, kj_r: (qi_r[s], h)
    kmap = lambda h, s, qi_r, kj_r: (kj_r[s], h)
    return pl.pallas_call(
        _sb_prompt_kernel,
        out_shape=jax.ShapeDtypeStruct((t_len, w), BF16),
        grid_spec=pltpu.PrefetchScalarGridSpec(
            num_scalar_prefetch=2,
            grid=(heads, len(qi)),
            in_specs=[pl.BlockSpec((blk, SB_HEAD_DIM), qmap),
                      pl.BlockSpec((blk, SB_HEAD_DIM), kmap),
                      pl.BlockSpec((blk, SB_HEAD_DIM), kmap),
                      pl.BlockSpec((blk, SB_HEAD_DIM), qmap),
                      pl.BlockSpec((1, 1, SB_HEAD_DIM), lambda h, s, qi_r, kj_r: (h, 0, 0)),
                      pl.BlockSpec((2 * SB_SUB, 2 * SB_SUB), lambda h, s, qi_r, kj_r: (0, 0))],
            out_specs=pl.BlockSpec((blk, SB_HEAD_DIM), qmap),
            scratch_shapes=[pltpu.VMEM((blk, SB_HEAD_DIM), F32),
                            pltpu.VMEM((blk, SB_HEAD_DIM), F32)]),
        compiler_params=_cparams(("arbitrary", "arbitrary")),
        name="sb_prompt",
    )(jnp.asarray(qi), jnp.asarray(kj), q16, k16, v16, uc, bias_b, _suffix_matrix())


SB_PAIR = 2
SB_SEQS = 4


def _sb_sample_kernel(pt_ref, q_ref, kc_ref, vc_ref, kn_ref, vn_ref, z_ref, bias_ref, uw_ref,
                      o_ref, kbuf, vbuf, sem, acc_ref, car_ref, kpg_ref, vpg_ref, *, n_tok, n_pages):
    n_seq, heads, rows, _ = q_ref.shape
    nsq = kbuf.shape[1]
    page_len = kbuf.shape[3]
    ngrp = heads // SB_PAIR
    total = (n_seq // nsq) * n_pages
    uw = uw_ref[...]
    bias = bias_ref[...]

    def page_copies(i, slot):
        i = jnp.minimum(i, total - 1)
        b0 = (i // n_pages) * nsq
        p = i % n_pages
        out = []
        for s in range(nsq):
            pg = pt_ref[(b0 + s) * n_pages + (n_pages - 1 - p)]
            for h in range(heads):
                out.append(pltpu.make_async_copy(kc_ref.at[pg, :, h, :], kbuf.at[slot, s, h],
                                                 sem.at[0, slot]))
                out.append(pltpu.make_async_copy(vc_ref.at[pg, :, h, :], vbuf.at[slot, s, h],
                                                 sem.at[1, slot]))
        return out

    def pages(b0, k_head, v_head, masked):
        zs = []
        for s in range(nsq):
            for g in range(ngrp):
                zg = None
                for h in range(g * SB_PAIR, (g + 1) * SB_PAIR):
                    zh = lax.dot_general(q_ref[b0 + s, h], k_head(s, h), (((1,), (1,)), ((), ())),
                                         preferred_element_type=F32)
                    zg = zh if zg is None else zg + zh
                zs.append(zg)
        z = jnp.concatenate(zs, axis=0) + bias
        mask = None
        if masked:
            tok = lax.broadcasted_iota(jnp.int32, z.shape, 0) % n_tok
            mask = lax.broadcasted_iota(jnp.int32, z.shape, 1) < tok
        a, car = _sb_weights(z, _sb_suffix(z, uw, mask), car_ref[...], mask)
        car_ref[...] = car
        a16 = a.astype(BF16)
        for s in range(nsq):
            for h in range(heads):
                r0 = (s * ngrp + h // SB_PAIR) * rows
                acc_ref[s, h] += jnp.dot(a16[r0:r0 + rows, :], v_head(s, h),
                                         preferred_element_type=F32)

    kpg_ref[...] = jnp.zeros_like(kpg_ref)
    vpg_ref[...] = jnp.zeros_like(vpg_ref)
    for c in page_copies(0, 0):
        c.start()

    def step(i, carry):
        slot = i % 2
        b0 = (i // n_pages) * nsq
        p = i % n_pages
        for c in page_copies(i + 1, 1 - slot):
            c.start()

        @pl.when(p == 0)
        def _():
            acc_ref[...] = jnp.zeros_like(acc_ref)
            car_ref[...] = jnp.zeros_like(car_ref)
            for s in range(nsq):
                kpg_ref[s, 0:kn_ref.shape[1], :] = kn_ref[b0 + s]
                vpg_ref[s, 0:vn_ref.shape[1], :] = vn_ref[b0 + s]
            pages(b0,
                  lambda s, h: kpg_ref[s, pl.ds(h, page_len, stride=heads), :].astype(BF16),
                  lambda s, h: vpg_ref[s, pl.ds(h, page_len, stride=heads), :].astype(BF16), True)

        for c in page_copies(i, slot):
            c.wait()
        pages(b0, lambda s, h: kbuf[slot, s, h].astype(BF16),
              lambda s, h: vbuf[slot, s, h].astype(BF16), False)

        @pl.when(p == n_pages - 1)
        def _():
            first = lax.broadcasted_iota(jnp.int32, (rows, SB_HEAD_DIM), 0) < n_tok
            for s in range(nsq):
                for g in range(ngrp):
                    og = jnp.where(first, acc_ref[s, g * SB_PAIR], acc_ref[s, g * SB_PAIR + 1])
                    zz = z_ref[b0 + s, g]
                    o_ref[b0 + s, g] = og * (zz * _sigmoid(zz))
        return carry

    lax.fori_loop(0, total, step, 0)
    for c in page_copies(total, total % 2):
        c.wait()


def _sb_sample(page_table, q8, cache_k, cache_v, k_new, v_new, z_g, bias_rows, n_tok):
    b, heads, rows, _ = q8.shape
    ngrp = heads // SB_PAIR
    n_pages = page_table.shape[1]
    page = cache_k.shape[1]
    nsq = SB_SEQS
    assert b % nsq == 0
    vmem = pl.BlockSpec(memory_space=pltpu.VMEM)
    hbm = pl.BlockSpec(memory_space=pl.ANY)
    return pl.pallas_call(
        functools.partial(_sb_sample_kernel, n_tok=n_tok, n_pages=n_pages),
        out_shape=jax.ShapeDtypeStruct((b, ngrp, rows, SB_HEAD_DIM), F32),
        in_specs=[pl.BlockSpec(memory_space=pltpu.SMEM), vmem, hbm, hbm,
                  vmem, vmem, vmem, vmem, vmem],
        out_specs=vmem,
        scratch_shapes=[pltpu.VMEM((2, nsq, heads, page, SB_HEAD_DIM), F32),
                        pltpu.VMEM((2, nsq, heads, page, SB_HEAD_DIM), F32),
                        pltpu.SemaphoreType.DMA((2, 2)),
                        pltpu.VMEM((nsq, heads, rows, SB_HEAD_DIM), F32),
                        pltpu.VMEM((nsq * ngrp * rows, SB_HEAD_DIM), F32),
                        pltpu.VMEM((nsq, page * heads, SB_HEAD_DIM), F32),
                        pltpu.VMEM((nsq, page * heads, SB_HEAD_DIM), F32)],
        compiler_params=pltpu.CompilerParams(vmem_limit_bytes=VMEM_LIMIT),
        name="sb_sample",
    )(page_table.reshape(-1), q8, cache_k, cache_v, k_new, v_new, z_g,
      jnp.tile(bias_rows, (nsq, 1)), _suffix_matrix())


def _pad_rows(a, rows):
    return jnp.pad(a, ((0, 0), (0, rows - a.shape[1]), (0, 0)))


def _layer(xp, xs, c_all, w, cache_k, cache_v, state, page_table, n_seq, n_tok):
    (w_ada, b_ada, norm_g, w_in, w_gate_up, b_gate, gla_norm_g,
     q_norm_g, k_norm_g, sb_bias, w_branch_a, w_branch_b, w_out) = w
    d = xp.shape[1]
    heads = d // SB_HEAD_DIM
    dv = gla_norm_g.shape[0]
    key_w = w_gate_up.shape[1]
    dk = key_w // GLA_HEADS
    val_w = GLA_HEADS * dv
    sb_w = heads * SB_HEAD_DIM

    c0 = 2 * key_w + 2 * val_w
    c1 = c0 + GLA_RANK
    w_a = w_in[:, :c0].astype(BF16)
    w_glr = w_in[:, c0:c1].astype(BF16)
    w_b = w_in[:, c1:].astype(BF16)
    wa16, wb16, wo16 = (w_branch_a.astype(BF16), w_branch_b.astype(BF16), w_out.astype(BF16))
    wg16 = w_gate_up.astype(BF16)

    mod = _ada(c_all, w_ada, b_ada)
    shift, scale, gate = mod[:, :d], mod[:, d:2 * d], mod[:, 2 * d:]
    rep = lambda a: jnp.repeat(a[:n_seq], n_tok, axis=0)
    q_scale = LOG2E / math.sqrt(SB_HEAD_DIM)

    def in_proj(h, tm, want_16):
        kv_types = (F32, BF16) if want_16 else (F32,)
        (ua,) = _proj(h, w_a, 0, c0, (F32,), tm, 1024, "proj_gla",
                      lead_cols=key_w, lead_scale=dk ** -0.5)
        (glr,) = _proj(h, w_glr, 0, GLA_RANK, (F32,), tm, GLA_RANK, "proj_glr")
        (q16,) = _proj(h, w_b, 0, sb_w, (BF16,), tm, 1024, "proj_qb",
                       norm_gain=q_norm_g, cast_scale=q_scale)
        k_out = _proj(h, w_b, sb_w, sb_w, kv_types, tm, 1024, "proj_kb", norm_gain=k_norm_g)
        v_out = _proj(h, w_b, 2 * sb_w, sb_w, kv_types, tm, 1024, "proj_vb")
        (uc,) = _proj(h, w_b, 3 * sb_w, sb_w + 2 * d, (F32,), tm, 1024, "proj_gates")
        return ua, glr, q16, k_out, v_out, uc

    hp = _mod_norm(xp, norm_g, scale[n_seq:n_seq + 1], shift[n_seq:n_seq + 1], 512)
    ua, glr, q16, (k32, k16), (v32, v16), uc = in_proj(hp, 1024, True)
    oa, s_p = _gla_prompt(ua, glr, wg16, b_gate, gla_norm_g, dk, dv)
    ob = _sb_prompt(q16, k16, v16, uc, sb_bias)
    y = _merge(oa, ob, wa16, wb16, uc, sb_w, sb_w + d, 512, 1024)
    out_p = _out_proj(y, wo16, xp, gate[n_seq:n_seq + 1], 1024, 1024)
    kp, vp = k32, v32

    m_s = xs.shape[0]
    hs = _mod_norm(xs, norm_g, rep(scale), rep(shift), m_s)
    ua, glr, q16, (k32,), (v32,), uc = in_proj(hs, m_s, False)
    ua3 = _pad_rows(ua.reshape(n_seq, n_tok, -1), GLA_PAD_ROWS)
    glr3 = _pad_rows(glr.reshape(n_seq, n_tok, -1), GLA_PAD_ROWS)
    oa3, s_s = _gla_sample(ua3, glr3, wg16, b_gate, gla_norm_g, state, n_tok)
    oa = oa3[:, :n_tok].reshape(m_s, val_w).astype(BF16)

    ngrp = heads // SB_PAIR
    qg = q16.reshape(n_seq, n_tok, ngrp, SB_PAIR, SB_HEAD_DIM).transpose(0, 2, 3, 1, 4)
    eye = jnp.eye(SB_PAIR, dtype=BF16)[None, None, :, :, None, None]
    q8 = (qg[:, :, :, None, :, :] * eye).reshape(n_seq, heads, SB_PAIR * n_tok, SB_HEAD_DIM)
    z_g = (uc[:, :sb_w].reshape(n_seq, n_tok, ngrp, SB_PAIR, SB_HEAD_DIM)
           .transpose(0, 2, 3, 1, 4).reshape(n_seq, ngrp, SB_PAIR * n_tok, SB_HEAD_DIM))
    bias_rows = jnp.broadcast_to(jnp.repeat(sb_bias.astype(F32) * LOG2E, n_tok)[:, None],
                                 (heads * n_tok, SB_HEAD_DIM))
    k_new = k32.reshape(n_seq, n_tok * heads, SB_HEAD_DIM)
    v_new = v32.reshape(n_seq, n_tok * heads, SB_HEAD_DIM)
    ob_g = _sb_sample(page_table, q8, cache_k, cache_v, k_new, v_new, z_g, bias_rows, n_tok)
    ob = (ob_g.reshape(n_seq, ngrp, SB_PAIR, n_tok, SB_HEAD_DIM).transpose(0, 3, 1, 2, 4)
          .reshape(m_s, sb_w).astype(BF16))
    y = _merge(oa, ob, wa16, wb16, uc, sb_w, sb_w + d, m_s, 1024)
    out_s = _out_proj(y, wo16, xs, rep(gate), m_s, 1024)
    return out_p, out_s, kp, vp, s_p, k32, v32, s_s


def kernel(x_prompt, x_sample, cache_k, cache_v, state_gla, page_table, c_prompt, c_sample,
           w_ada, b_ada, norm_g, w_in, w_gate_up, b_gate, gla_norm_g, q_norm_g, k_norm_g,
           sb_bias, w_branch_a, w_branch_b, w_out):
    depth = w_ada.shape[0]
    n_b, t_len, d = x_prompt.shape
    assert n_b == 1, "the prompt group is a single sequence"
    n_seq, n_tok, _ = x_sample.shape
    heads = d // SB_HEAD_DIM
    pad = (-(n_seq + n_b)) % 8
    c_all = jnp.concatenate([c_sample, c_prompt, jnp.zeros((pad, d), F32)], axis=0)
    xp = x_prompt.reshape(t_len, d)
    xs = x_sample.reshape(n_seq * n_tok, d)
    outs = [[] for _ in range(6)]
    for l in range(depth):
        w = (w_ada[l], b_ada[l], norm_g[l], w_in[l], w_gate_up[l], b_gate[l], gla_norm_g[l],
             q_norm_g[l], k_norm_g[l], sb_bias[l], w_branch_a[l], w_branch_b[l], w_out[l])
        xp, xs, kp, vp, sp, ks, vs, ss = _layer(xp, xs, c_all, w, cache_k[l], cache_v[l],
                                                 state_gla[l], page_table, n_seq, n_tok)
        outs[0].append(kp.reshape(n_b, t_len, heads, SB_HEAD_DIM))
        outs[1].append(vp.reshape(n_b, t_len, heads, SB_HEAD_DIM))
        outs[2].append(sp[None])
        outs[3].append(ks.reshape(n_seq, n_tok, heads, SB_HEAD_DIM))
        outs[4].append(vs.reshape(n_seq, n_tok, heads, SB_HEAD_DIM))
        outs[5].append(ss)
    k_p, v_p, s_p, k_s, v_s, s_s = (jnp.stack(o) for o in outs)
    return (xp.reshape(n_b, t_len, d), xs.reshape(n_seq, n_tok, d), k_p, v_p, s_p, k_s, v_s, s_s)
```

```python
import functools
import math

import numpy as np
import jax
import jax.numpy as jnp
from jax import lax
from jax.experimental import pallas as pl
from jax.experimental.pallas import tpu as pltpu

F32 = jnp.float32
BF16 = jnp.bfloat16

GLA_HEADS = 4
GLA_RANK = 16
GLA_TAU = 16.0
SB_HEAD_DIM = 128
NORM_EPS = 1e-6
LOG2E = 1.4426950408889634
LN2 = 0.6931471805599453

LANES = 128
VMEM_LIMIT = 56 * 1024 * 1024


def _cparams(sem):
    return pltpu.CompilerParams(dimension_semantics=sem, vmem_limit_bytes=VMEM_LIMIT)


def _sigmoid(x):
    return 1.0 / (1.0 + jnp.exp(-x))


def _softplus_parts(z):
    l = jnp.log(1.0 + jnp.exp(-jnp.abs(z)))
    return jnp.maximum(z, 0.0) + l, l


def _ada_kernel(c_ref, w_ref, b_ref, o_ref):
    c = c_ref[...]
    s = (c * _sigmoid(c)).astype(BF16)
    o_ref[...] = jnp.dot(s, w_ref[...].astype(BF16), preferred_element_type=F32) + b_ref[...]


def _ada(c_all, w_ada, b_ada, tn=512):
    r, d = c_all.shape
    n = w_ada.shape[1]
    return pl.pallas_call(
        _ada_kernel,
        out_shape=jax.ShapeDtypeStruct((r, n), F32),
        grid=(n // tn,),
        in_specs=[pl.BlockSpec((r, d), lambda j: (0, 0)),
                  pl.BlockSpec((d, tn), lambda j: (0, j)),
                  pl.BlockSpec((1, tn), lambda j: (0, j))],
        out_specs=pl.BlockSpec((r, tn), lambda j: (0, j)),
        compiler_params=_cparams(("arbitrary",)),
        name="ada_mod",
    )(c_all, w_ada, b_ada.reshape(1, n))


def _norm_kernel(x_ref, g_ref, scale_ref, shift_ref, h_ref):
    x = x_ref[...]
    ms = jnp.mean(x * x, axis=-1, keepdims=True)
    y = x * lax.rsqrt(ms + NORM_EPS) * g_ref[...]
    h_ref[...] = (y * (1.0 + scale_ref[...]) + shift_ref[...]).astype(h_ref.dtype)


def _mod_norm(x, g, scale, shift, tm):
    m, d = x.shape
    per_row = scale.shape[0] != 1
    mod_spec = (pl.BlockSpec((tm, d), lambda i: (i, 0)) if per_row
                else pl.BlockSpec((1, d), lambda i: (0, 0)))
    return pl.pallas_call(
        _norm_kernel,
        out_shape=jax.ShapeDtypeStruct((m, d), BF16),
        grid=(m // tm,),
        in_specs=[pl.BlockSpec((tm, d), lambda i: (i, 0)),
                  pl.BlockSpec((1, d), lambda i: (0, 0)),
                  mod_spec, mod_spec],
        out_specs=pl.BlockSpec((tm, d), lambda i: (i, 0)),
        compiler_params=_cparams(("arbitrary",)),
        name="mod_norm",
    )(x, g.reshape(1, d), scale, shift)


def _cast_cols_kernel(a_ref, b_ref, o_ref, *, off):
    x = jnp.concatenate([a_ref[...], b_ref[...]], axis=1)
    o_ref[...] = x[:, off:off + o_ref.shape[1]].astype(o_ref.dtype)


def _cast_cols(w, col0, n, tr=256, tc=1024):
    rows = w.shape[0]
    base = (col0 // LANES) * LANES
    off = col0 - base
    assert base % tc == 0 and n % tc == 0 and rows % tr == 0 and 0 < off < LANES
    ja = base // tc
    jb = base // LANES
    return pl.pallas_call(
        functools.partial(_cast_cols_kernel, off=off),
        out_shape=jax.ShapeDtypeStruct((rows, n), BF16),
        grid=(rows // tr, n // tc),
        in_specs=[pl.BlockSpec((tr, tc), lambda i, j: (i, ja + j)),
                  pl.BlockSpec((tr, LANES), lambda i, j: (i, jb + (j + 1) * (tc // LANES)))],
        out_specs=pl.BlockSpec((tr, tc), lambda i, j: (i, j)),
        compiler_params=_cparams(("arbitrary", "arbitrary")),
        name="cast_cols",
    )(w, w)


def _proj_kernel(h_ref, w_ref, *refs, lead_tiles, lead_scale, head_norm, cast_scale):
    o_refs = refs[1:] if head_norm else refs
    acc = jnp.dot(h_ref[...], w_ref[...], preferred_element_type=F32)
    if lead_tiles:
        acc = acc * jnp.where(pl.program_id(1) < lead_tiles, lead_scale, 1.0)

    def emit(cols, y):
        for o_ref in o_refs:
            o_ref[:, cols] = y if o_ref.dtype == F32 else (y * cast_scale).astype(o_ref.dtype)

    if head_norm:
        g = refs[0][...]
        for c in range(acc.shape[1] // SB_HEAD_DIM):
            cols = slice(c * SB_HEAD_DIM, (c + 1) * SB_HEAD_DIM)
            blk = acc[:, cols]
            ms = jnp.mean(blk * blk, axis=-1, keepdims=True)
            emit(cols, blk * lax.rsqrt(ms + NORM_EPS) * g)
    else:
        emit(slice(None), acc)


def _proj(h, w, col0, n, out_dtypes, tm, tn, name, *, lead_cols=0, lead_scale=1.0,
          norm_gain=None, cast_scale=1.0):
    m, k = h.shape
    tn = min(tn, n)
    j0 = col0 // tn
    assert col0 % tn == 0 and n % tn == 0 and m % tm == 0 and lead_cols % tn == 0
    in_specs = [pl.BlockSpec((tm, k), lambda i, j: (i, 0)),
                pl.BlockSpec((k, tn), lambda i, j: (0, j0 + j))]
    args = [h, w]
    if norm_gain is not None:
        in_specs.append(pl.BlockSpec((1, SB_HEAD_DIM), lambda i, j: (0, 0)))
        args.append(norm_gain.reshape(1, SB_HEAD_DIM))
    tile = pl.BlockSpec((tm, tn), lambda i, j: (i, j))
    return pl.pallas_call(
        functools.partial(_proj_kernel, lead_tiles=lead_cols // tn, lead_scale=lead_scale,
                          head_norm=norm_gain is not None, cast_scale=cast_scale),
        out_shape=tuple(jax.ShapeDtypeStruct((m, n), dt) for dt in out_dtypes),
        grid=(m // tm, n // tn),
        in_specs=in_specs,
        out_specs=tuple(tile for _ in out_dtypes),
        compiler_params=_cparams(("arbitrary", "arbitrary")),
        name=name,
    )(*args)


def _merge_kernel(oa_ref, ob_ref, wa_ref, wb_ref, ga_ref, gb_ref, y_ref):
    ya = jnp.dot(oa_ref[...], wa_ref[...], preferred_element_type=F32)
    yb = jnp.dot(ob_ref[...], wb_ref[...], preferred_element_type=F32)
    y_ref[...] = (_sigmoid(ga_ref[...]) * ya + _sigmoid(gb_ref[...]) * yb).astype(y_ref.dtype)


def _merge(oa, ob, wa, wb, uc, ga_col, gb_col, tm, tn):
    m, k = oa.shape
    n = wa.shape[1]
    row = lambda i, j: (i, 0)
    col = lambda i, j: (0, j)
    ja, jb = ga_col // tn, gb_col // tn
    return pl.pallas_call(
        _merge_kernel,
        out_shape=jax.ShapeDtypeStruct((m, n), BF16),
        grid=(m // tm, n // tn),
        in_specs=[pl.BlockSpec((tm, k), row), pl.BlockSpec((tm, k), row),
                  pl.BlockSpec((k, tn), col), pl.BlockSpec((k, tn), col),
                  pl.BlockSpec((tm, tn), lambda i, j: (i, ja + j)),
                  pl.BlockSpec((tm, tn), lambda i, j: (i, jb + j))],
        out_specs=pl.BlockSpec((tm, tn), lambda i, j: (i, j)),
        compiler_params=_cparams(("arbitrary", "arbitrary")),
        name="merge",
    )(oa, ob, wa, wb, uc, uc)


def _out_kernel(y_ref, w_ref, x_ref, gate_ref, o_ref):
    acc = jnp.dot(y_ref[...], w_ref[...], preferred_element_type=F32)
    o_ref[...] = x_ref[...] + gate_ref[...] * acc


def _out_proj(y, w, x, gate, tm, tn):
    m, k = y.shape
    n = w.shape[1]
    per_row = gate.shape[0] != 1
    gate_spec = (pl.BlockSpec((tm, tn), lambda i, j: (i, j)) if per_row
                 else pl.BlockSpec((1, tn), lambda i, j: (0, j)))
    return pl.pallas_call(
        _out_kernel,
        out_shape=jax.ShapeDtypeStruct((m, n), F32),
        grid=(m // tm, n // tn),
        in_specs=[pl.BlockSpec((tm, k), lambda i, j: (i, 0)),
                  pl.BlockSpec((k, tn), lambda i, j: (0, j)),
                  pl.BlockSpec((tm, tn), lambda i, j: (i, j)),
                  gate_spec],
        out_specs=pl.BlockSpec((tm, tn), lambda i, j: (i, j)),
        compiler_params=_cparams(("arbitrary", "arbitrary")),
        name="out_proj",
    )(y, w, x, gate)


GLA_SUB = 16
GLA_BLK = 64
GLA_ROWS = 256


def _split3(x):
    hi = x.astype(BF16)
    r1 = x - hi.astype(F32)
    mid = r1.astype(BF16)
    lo = (r1 - mid.astype(F32)).astype(BF16)
    return hi, mid, lo


def _gla_prompt_kernel(q_ref, k_ref, v_ref, z_ref, glr_ref, wg_ref, bg_ref, gn_ref, tri_ref,
                       o_ref, s_ref, st_ref, b_ref):
    t = pl.program_id(0)
    nt = pl.num_programs(0)
    rows = q_ref.shape[0]
    nh, dk, dv = st_ref.shape
    c = GLA_SUB
    blk = min(GLA_BLK, rows)

    @pl.when(t == 0)
    def _():
        st_ref[...] = jnp.zeros_like(st_ref)

    x = jnp.dot(glr_ref[...].astype(BF16), wg_ref[...], preferred_element_type=F32) + bg_ref[...]
    la = (jnp.minimum(x, 0.0) - jnp.log(1.0 + jnp.exp(-jnp.abs(x)))) * (1.0 / GLA_TAU)
    tri = tri_ref[...]
    hi, mid, lo = _split3(la)
    b_ref[...] = (jnp.dot(tri, hi, preferred_element_type=F32)
                  + jnp.dot(tri, mid, preferred_element_type=F32)
                  + jnp.dot(tri, lo, preferred_element_type=F32))

    row_id = lax.broadcasted_iota(jnp.int32, (c // 2, 1), 0)
    lane_id = lax.broadcasted_iota(jnp.int32, (c // 2, c), 1)
    gn = gn_ref[...]

    heads = range(nh)
    ksl = [slice(h * dk, (h + 1) * dk) for h in heads]
    vsl = [slice(h * dv, (h + 1) * dv) for h in heads]
    nt_dims = (((1,), (1,)), ((), ()))

    def sub_chunk(r0, i, st16):
        ri = r0 + i * c
        rs = pl.ds(ri, c)
        qc = [q_ref[rs, ksl[h]] for h in heads]
        bc = [b_ref[rs, ksl[h]] for h in heads]
        o_acc = [jnp.dot((qc[h] * jnp.exp(bc[h])).astype(BF16), st16[h],
                         preferred_element_type=F32) for h in heads]
        if i > 0:
            prev = pl.ds(r0, i * c)
            for h in heads:
                b_ref0 = b_ref[pl.ds(ri - 1, 1), ksl[h]]
                qq = (qc[h] * jnp.exp(bc[h] - b_ref0)).astype(BF16)
                kk = (k_ref[prev, ksl[h]] * jnp.exp(b_ref0 - b_ref[prev, ksl[h]])).astype(BF16)
                a_off = lax.dot_general(qq, kk, nt_dims, preferred_element_type=F32)
                o_acc[h] = o_acc[h] + jnp.dot(a_off.astype(BF16), v_ref[prev, vsl[h]].astype(BF16),
                                              preferred_element_type=F32)
        half = c // 2
        a_top = [jnp.zeros((half, c), F32) for _ in heads]
        a_bot = [jnp.zeros((half, c), F32) for _ in heads]
        for s in range(c):
            for h in heads:
                ks = k_ref[pl.ds(ri + s, 1), ksl[h]]
                bs = b_ref[pl.ds(ri + s, 1), ksl[h]]
                if s < half:
                    w = jnp.where(row_id >= s,
                                  qc[h][:half] * ks * jnp.exp(bc[h][:half] - bs), 0.0)
                    a_top[h] = jnp.where(lane_id == s, jnp.sum(w, axis=-1, keepdims=True), a_top[h])
                    w = qc[h][half:] * ks * jnp.exp(bc[h][half:] - bs)
                else:
                    w = jnp.where(row_id >= s - half,
                                  qc[h][half:] * ks * jnp.exp(bc[h][half:] - bs), 0.0)
                a_bot[h] = jnp.where(lane_id == s, jnp.sum(w, axis=-1, keepdims=True), a_bot[h])
        for h in heads:
            a = jnp.concatenate([a_top[h], a_bot[h]], axis=0).astype(BF16)
            o = o_acc[h] + jnp.dot(a, v_ref[rs, vsl[h]].astype(BF16), preferred_element_type=F32)
            ms = jnp.mean(o * o, axis=-1, keepdims=True)
            zc = z_ref[rs, vsl[h]]
            y = o * lax.rsqrt(ms + NORM_EPS) * gn * (zc * _sigmoid(zc))
            o_ref[rs, vsl[h]] = y.astype(o_ref.dtype)

    def block(bi, carry):
        r0 = pl.multiple_of(bi * blk, blk)
        rb = pl.ds(r0, blk)
        st = [st_ref[h] for h in heads]
        st16 = [s.astype(BF16) for s in st]
        for i in range(blk // c):
            sub_chunk(r0, i, st16)
        for h in heads:
            bb = b_ref[rb, ksl[h]]
            kd = (k_ref[rb, ksl[h]] * jnp.exp(bb[blk - 1:blk] - bb)).astype(BF16)
            upd = lax.dot_general(kd, v_ref[rb, vsl[h]].astype(BF16), (((0,), (0,)), ((), ())),
                                  preferred_element_type=F32)
            st_ref[h] = st[h] * jnp.exp(bb[blk - 8:].T[:, 7:8]) + upd
        return carry

    lax.fori_loop(0, rows // blk, block, 0)

    @pl.when(t == nt - 1)
    def _():
        s_ref[...] = st_ref[...]


def _gla_prompt(ua, glr, wg, bg, gn, dk, dv):
    t_len = ua.shape[0]
    h = GLA_HEADS
    rows = min(GLA_ROWS, t_len)
    r = np.arange(rows)
    tri = ((r[:, None] >= r[None, :]) & (r[:, None] // GLA_BLK == r[None, :] // GLA_BLK))
    tri = jnp.asarray(tri, dtype=BF16)
    v_blk = 2 * h * dk // (h * dv)
    return pl.pallas_call(
        _gla_prompt_kernel,
        out_shape=(jax.ShapeDtypeStruct((t_len, h * dv), BF16),
                   jax.ShapeDtypeStruct((h, dk, dv), F32)),
        grid=(t_len // rows,),
        in_specs=[pl.BlockSpec((rows, h * dk), lambda tt: (tt, 0)),
                  pl.BlockSpec((rows, h * dk), lambda tt: (tt, 1)),
                  pl.BlockSpec((rows, h * dv), lambda tt: (tt, v_blk)),
                  pl.BlockSpec((rows, h * dv), lambda tt: (tt, v_blk + 1)),
                  pl.BlockSpec((rows, GLA_RANK), lambda tt: (tt, 0)),
                  pl.BlockSpec((GLA_RANK, h * dk), lambda tt: (0, 0)),
                  pl.BlockSpec((1, h * dk), lambda tt: (0, 0)),
                  pl.BlockSpec((1, dv), lambda tt: (0, 0)),
                  pl.BlockSpec((rows, rows), lambda tt: (0, 0))],
        out_specs=(pl.BlockSpec((rows, h * dv), lambda tt: (tt, 0)),
                   pl.BlockSpec((h, dk, dv), lambda tt: (0, 0, 0))),
        scratch_shapes=[pltpu.VMEM((h, dk, dv), F32), pltpu.VMEM((rows, h * dk), F32)],
        compiler_params=_cparams(("arbitrary",)),
        name="gla_prompt",
    )(ua, ua, ua, ua, glr, wg, bg.reshape(1, -1), gn.reshape(1, -1), tri)


GLA_PAD_ROWS = 8


def _log_decay(x):
    return (jnp.minimum(x, 0.0) - jnp.log(1.0 + jnp.exp(-jnp.abs(x)))) * (1.0 / GLA_TAU)


def _gla_sample_kernel(q_ref, k_ref, v_ref, z_ref, glr_ref, wg_ref, bg_ref, gn_ref, s_ref,
                       o_ref, so_ref, *, n_tok):
    rows = q_ref.shape[1]
    dk = s_ref.shape[2]
    dv = s_ref.shape[3]
    row_id = lax.broadcasted_iota(jnp.int32, (rows, 1), 0)
    gn = gn_ref[...]
    glr = glr_ref[0].astype(BF16)
    heads = range(GLA_HEADS)
    ksl = [slice(h * dk, (h + 1) * dk) for h in heads]
    vsl = [slice(h * dv, (h + 1) * dv) for h in heads]
    la = _log_decay(jnp.dot(glr, wg_ref[...], preferred_element_type=F32) + bg_ref[...])
    b_all = jnp.zeros_like(la)
    for s in range(n_tok):
        b_all = b_all + jnp.where(row_id >= s, la[s:s + 1, :], 0.0)
    bc = [b_all[:, ksl[h]] for h in heads]
    qc = [q_ref[0, :, ksl[h]] for h in heads]
    kc = [k_ref[0, :, ksl[h]] for h in heads]
    vc = [v_ref[0, :, vsl[h]] for h in heads]
    st = [s_ref[0, h] for h in heads]
    o = [jnp.dot((qc[h] * jnp.exp(bc[h])).astype(BF16), st[h].astype(BF16),
                 preferred_element_type=F32) for h in heads]
    upd = [lax.dot_general((kc[h] * jnp.exp(bc[h][n_tok - 1:n_tok] - bc[h])).astype(BF16),
                           vc[h].astype(BF16), (((0,), (0,)), ((), ())),
                           preferred_element_type=F32) for h in heads]
    decay_col = [jnp.exp(bc[h].T[:, n_tok - 1:n_tok]) for h in heads]
    for s in range(n_tok):
        for h in heads:
            w = jnp.where(row_id >= s,
                          qc[h] * kc[h][s:s + 1] * jnp.exp(bc[h] - bc[h][s:s + 1]), 0.0)
            o[h] = o[h] + jnp.sum(w, axis=-1, keepdims=True) * vc[h][s:s + 1]
    for h in heads:
        ms = jnp.mean(o[h] * o[h], axis=-1, keepdims=True)
        zc = z_ref[0, :, vsl[h]]
        o_ref[0, :, vsl[h]] = o[h] * lax.rsqrt(ms + NORM_EPS) * gn * (zc * _sigmoid(zc))
        so_ref[0, h] = st[h] * decay_col[h] + upd[h]


def _gla_sample(ua, glr, wg, bg, gn, state, n_tok):
    b, rows, _ = ua.shape
    h, dk, dv = state.shape[1:]
    k_blk = h * dk // (h * dk)
    v_blk = 2 * h * dk // (h * dv)
    z_blk = v_blk + 1
    return pl.pallas_call(
        functools.partial(_gla_sample_kernel, n_tok=n_tok),
        out_shape=(jax.ShapeDtypeStruct((b, rows, h * dv), F32),
                   jax.ShapeDtypeStruct(state.shape, F32)),
        grid=(b,),
        in_specs=[pl.BlockSpec((1, rows, h * dk), lambda i: (i, 0, 0)),
                  pl.BlockSpec((1, rows, h * dk), lambda i: (i, 0, k_blk)),
                  pl.BlockSpec((1, rows, h * dv), lambda i: (i, 0, v_blk)),
                  pl.BlockSpec((1, rows, h * dv), lambda i: (i, 0, z_blk)),
                  pl.BlockSpec((1, rows, GLA_RANK), lambda i: (i, 0, 0)),
                  pl.BlockSpec((GLA_RANK, h * dk), lambda i: (0, 0)),
                  pl.BlockSpec((1, h * dk), lambda i: (0, 0)),
                  pl.BlockSpec((1, dv), lambda i: (0, 0)),
                  pl.BlockSpec((1, h, dk, dv), lambda i: (i, 0, 0, 0))],
        out_specs=(pl.BlockSpec((1, rows, h * dv), lambda i: (i, 0, 0)),
                   pl.BlockSpec((1, h, dk, dv), lambda i: (i, 0, 0, 0))),
        compiler_params=_cparams(("arbitrary",)),
        name="gla_sample",
    )(ua, ua, ua, ua, glr, wg, bg.reshape(1, -1), gn.reshape(1, -1), state)


SB_BLK = 2048
SB_SUB = 128


def _suffix_matrix():
    j = np.arange(SB_SUB)
    u = (j[:, None] >= j[None, :]).astype(np.float32)
    half = np.concatenate([u, np.ones((SB_SUB, SB_SUB), np.float32)], axis=1)
    return jnp.asarray(np.concatenate([half, half], axis=0), dtype=BF16)


def _sb_suffix(z, uw, mask):
    neg_abs = lax.bitcast_convert_type(
        lax.bitcast_convert_type(z, jnp.int32) | jnp.int32(-2 ** 31), F32)
    sp = jnp.maximum(z, 0.0) + jnp.log(1.0 + jnp.exp2(neg_abs)) * LOG2E
    if mask is not None:
        sp = jnp.where(mask, sp, 0.0)
    hi = sp.astype(BF16)
    lo = (sp - hi.astype(F32)).astype(BF16)
    return jnp.dot(jnp.concatenate([hi, lo], axis=1), uw, preferred_element_type=F32)


def _sb_weights(z, s2, car, mask):
    a = jnp.exp2(z - s2[:, :SB_SUB] - car)
    if mask is not None:
        a = jnp.where(mask, a, 0.0)
    return a, car + s2[:, SB_SUB:]


def _sb_prompt_kernel(qi_ref, kj_ref, q_ref, k_ref, v_ref, z_ref, bias_ref, uw_ref,
                      o_ref, acc_ref, car_ref):
    step = pl.program_id(1)
    qi = qi_ref[step]
    kj = kj_ref[step]
    blk = q_ref.shape[0]
    nsub = blk // SB_SUB
    uw = uw_ref[...]
    bias = bias_ref[0]

    @pl.when(kj == qi)
    def _():
        acc_ref[...] = jnp.zeros_like(acc_ref)
        car_ref[...] = jnp.zeros_like(car_ref)

    def logits(ks, r0):
        return lax.dot_general(q_ref[r0:blk, :], k_ref[ks * SB_SUB:(ks + 1) * SB_SUB, :],
                               (((1,), (1,)), ((), ())), preferred_element_type=F32) + bias

    def causal_mask(shape, masked):
        if not masked:
            return None
        return lax.broadcasted_iota(jnp.int32, shape, 1) < lax.broadcasted_iota(jnp.int32, shape, 0)

    def finish(ks, r0, z, s2, mask):
        a, car = _sb_weights(z, s2, car_ref[r0:blk, :], mask)
        car_ref[r0:blk, :] = car
        acc_ref[r0:blk, :] += jnp.dot(a.astype(BF16), v_ref[ks * SB_SUB:(ks + 1) * SB_SUB, :],
                                      preferred_element_type=F32)

    def sweep(masked):
        tiles = [(ks, ks * SB_SUB if masked else 0) for ks in range(nsub - 1, -1, -1)]
        z = logits(*tiles[0])
        mask = causal_mask(z.shape, masked)
        s2 = _sb_suffix(z, uw, mask)
        for i, (ks, r0) in enumerate(tiles):
            nxt = tiles[i + 1] if i + 1 < len(tiles) else None
            if nxt is not None:
                z_n = logits(*nxt)
                mask_n = causal_mask(z_n.shape, masked)
            finish(ks, r0, z, s2, mask)
            if nxt is not None:
                s2 = _sb_suffix(z_n, uw, mask_n)
                z, mask = z_n, mask_n

    @pl.when(kj == qi)
    def _():
        sweep(True)

    @pl.when(kj != qi)
    def _():
        sweep(False)

    @pl.when(kj == 0)
    def _():
        zz = z_ref[...]
        o_ref[...] = (acc_ref[...] * (zz * _sigmoid(zz))).astype(o_ref.dtype)


def _sb_prompt(q16, k16, v16, uc, bias):
    t_len, w = q16.shape
    heads = w // SB_HEAD_DIM
    blk = min(SB_BLK, t_len)
    nq = t_len // blk
    qi = np.concatenate([np.full(i + 1, i) for i in range(nq)]).astype(np.int32)
    kj = np.concatenate([np.arange(i, -1, -1) for i in range(nq)]).astype(np.int32)
    bias_b = jnp.broadcast_to((bias.astype(F32) * LOG2E)[:, None, None], (heads, 1, SB_HEAD_DIM))
    qmap = lambda h, s, qi_r, kj_r: (qi_r[s], h)
    kmap = lambda h, s, qi_r, kj_r: (kj_r[s], h)
    return pl.pallas_call(
        _sb_prompt_kernel,
        out_shape=jax.ShapeDtypeStruct((t_len, w), BF16),
        grid_spec=pltpu.PrefetchScalarGridSpec(
            num_scalar_prefetch=2,
            grid=(heads, len(qi)),
            in_specs=[pl.BlockSpec((blk, SB_HEAD_DIM), qmap),
                      pl.BlockSpec((blk, SB_HEAD_DIM), kmap),
                      pl.BlockSpec((blk, SB_HEAD_DIM), kmap),
                      pl.BlockSpec((blk, SB_HEAD_DIM), qmap),
                      pl.BlockSpec((1, 1, SB_HEAD_DIM), lambda h, s, qi_r, kj_r: (h, 0, 0)),
                      pl.BlockSpec((2 * SB_SUB, 2 * SB_SUB), lambda h, s, qi_r, kj_r: (0, 0))],
            out_specs=pl.BlockSpec((blk, SB_HEAD_DIM), qmap),
            scratch_shapes=[pltpu.VMEM((blk, SB_HEAD_DIM), F32),
                            pltpu.VMEM((blk, SB_HEAD_DIM), F32)]),
        compiler_params=_cparams(("arbitrary", "arbitrary")),
        name="sb_prompt",
    )(jnp.asarray(qi), jnp.asarray(kj), q16, k16, v16, uc, bias_b, _suffix_matrix())


SB_PAIR = 2
SB_SEQS = 4


def _sb_sample_kernel(pt_ref, q_ref, kc_ref, vc_ref, kn_ref, vn_ref, z_ref, bias_ref, uw_ref,
                      o_ref, kbuf, vbuf, sem, acc_ref, car_ref, kpg_ref, vpg_ref, *, n_tok, n_pages):
    n_seq, heads, rows, _ = q_ref.shape
    nsq = kbuf.shape[1]
    page_len = kbuf.shape[3]
    ngrp = heads // SB_PAIR
    total = (n_seq // nsq) * n_pages
    uw = uw_ref[...]
    bias = bias_ref[...]

    def page_copies(i, slot):
        i = jnp.minimum(i, total - 1)
        b0 = (i // n_pages) * nsq
        p = i % n_pages
        out = []
        for s in range(nsq):
            pg = pt_ref[(b0 + s) * n_pages + (n_pages - 1 - p)]
            for h in range(heads):
                out.append(pltpu.make_async_copy(kc_ref.at[pg, :, h, :], kbuf.at[slot, s, h],
                                                 sem.at[0, slot]))
                out.append(pltpu.make_async_copy(vc_ref.at[pg, :, h, :], vbuf.at[slot, s, h],
                                                 sem.at[1, slot]))
        return out

    def pages(b0, k_head, v_head, masked):
        zs = []
        for s in range(nsq):
            for g in range(ngrp):
                zg = None
                for h in range(g * SB_PAIR, (g + 1) * SB_PAIR):
                    zh = lax.dot_general(q_ref[b0 + s, h], k_head(s, h), (((1,), (1,)), ((), ())),
                                         preferred_element_type=F32)
                    zg = zh if zg is None else zg + zh
                zs.append(zg)
        z = jnp.concatenate(zs, axis=0) + bias
        mask = None
        if masked:
            tok = lax.broadcasted_iota(jnp.int32, z.shape, 0) % n_tok
            mask = lax.broadcasted_iota(jnp.int32, z.shape, 1) < tok
        a, car = _sb_weights(z, _sb_suffix(z, uw, mask), car_ref[...], mask)
        car_ref[...] = car
        a16 = a.astype(BF16)
        for s in range(nsq):
            for h in range(heads):
                r0 = (s * ngrp + h // SB_PAIR) * rows
                acc_ref[s, h] += jnp.dot(a16[r0:r0 + rows, :], v_head(s, h),
                                         preferred_element_type=F32)

    kpg_ref[...] = jnp.zeros_like(kpg_ref)
    vpg_ref[...] = jnp.zeros_like(vpg_ref)
    for c in page_copies(0, 0):
        c.start()

    def step(i, carry):
        slot = i % 2
        b0 = (i // n_pages) * nsq
        p = i % n_pages
        for c in page_copies(i + 1, 1 - slot):
            c.start()

        @pl.when(p == 0)
        def _():
            acc_ref[...] = jnp.zeros_like(acc_ref)
            car_ref[...] = jnp.zeros_like(car_ref)
            for s in range(nsq):
                kpg_ref[s, 0:kn_ref.shape[1], :] = kn_ref[b0 + s]
                vpg_ref[s, 0:vn_ref.shape[1], :] = vn_ref[b0 + s]
            pages(b0,
                  lambda s, h: kpg_ref[s, pl.ds(h, page_len, stride=heads), :].astype(BF16),
                  lambda s, h: vpg_ref[s, pl.ds(h, page_len, stride=heads), :].astype(BF16), True)

        for c in page_copies(i, slot):
            c.wait()
        pages(b0, lambda s, h: kbuf[slot, s, h].astype(BF16),
              lambda s, h: vbuf[slot, s, h].astype(BF16), False)

        @pl.when(p == n_pages - 1)
        def _():
            first = lax.broadcasted_iota(jnp.int32, (rows, SB_HEAD_DIM), 0) < n_tok
            for s in range(nsq):
                for g in range(ngrp):
                    og = jnp.where(first, acc_ref[s, g * SB_PAIR], acc_ref[s, g * SB_PAIR + 1])
                    zz = z_ref[b0 + s, g]
                    o_ref[b0 + s, g] = og * (zz * _sigmoid(zz))
        return carry

    lax.fori_loop(0, total, step, 0)
    for c in page_copies(total, total % 2):
        c.wait()


def _sb_sample(page_table, q8, cache_k, cache_v, k_new, v_new, z_g, bias_rows, n_tok):
    b, heads, rows, _ = q8.shape
    ngrp = heads // SB_PAIR
    n_pages = page_table.shape[1]
    page = cache_k.shape[1]
    nsq = SB_SEQS
    assert b % nsq == 0
    vmem = pl.BlockSpec(memory_space=pltpu.VMEM)
    hbm = pl.BlockSpec(memory_space=pl.ANY)
    return pl.pallas_call(
        functools.partial(_sb_sample_kernel, n_tok=n_tok, n_pages=n_pages),
        out_shape=jax.ShapeDtypeStruct((b, ngrp, rows, SB_HEAD_DIM), F32),
        in_specs=[pl.BlockSpec(memory_space=pltpu.SMEM), vmem, hbm, hbm,
                  vmem, vmem, vmem, vmem, vmem],
        out_specs=vmem,
        scratch_shapes=[pltpu.VMEM((2, nsq, heads, page, SB_HEAD_DIM), F32),
                        pltpu.VMEM((2, nsq, heads, page, SB_HEAD_DIM), F32),
                        pltpu.SemaphoreType.DMA((2, 2)),
                        pltpu.VMEM((nsq, heads, rows, SB_HEAD_DIM), F32),
                        pltpu.VMEM((nsq * ngrp * rows, SB_HEAD_DIM), F32),
                        pltpu.VMEM((nsq, page * heads, SB_HEAD_DIM), F32),
                        pltpu.VMEM((nsq, page * heads, SB_HEAD_DIM), F32)],
        compiler_params=pltpu.CompilerParams(vmem_limit_bytes=VMEM_LIMIT),
        name="sb_sample",
    )(page_table.reshape(-1), q8, cache_k, cache_v, k_new, v_new, z_g,
      jnp.tile(bias_rows, (nsq, 1)), _suffix_matrix())


def _pad_rows(a, rows):
    return jnp.pad(a, ((0, 0), (0, rows - a.shape[1]), (0, 0)))


def _layer(xp, xs, c_all, w, cache_k, cache_v, state, page_table, n_seq, n_tok):
    (w_ada, b_ada, norm_g, w_in, w_gate_up, b_gate, gla_norm_g,
     q_norm_g, k_norm_g, sb_bias, w_branch_a, w_branch_b, w_out) = w
    d = xp.shape[1]
    heads = d // SB_HEAD_DIM
    dv = gla_norm_g.shape[0]
    key_w = w_gate_up.shape[1]
    dk = key_w // GLA_HEADS
    val_w = GLA_HEADS * dv
    sb_w = heads * SB_HEAD_DIM

    c0 = 2 * key_w + 2 * val_w
    c1 = c0 + GLA_RANK
    w_a = w_in[:, :c0].astype(BF16)
    w_glr = w_in[:, c0:c1].astype(BF16)
    w_b = _cast_cols(w_in, c1, w_in.shape[1] - c1)
    wa16, wb16, wo16 = (w_branch_a.astype(BF16), w_branch_b.astype(BF16), w_out.astype(BF16))
    wg16 = w_gate_up.astype(BF16)

    mod = _ada(c_all, w_ada, b_ada)
    shift, scale, gate = mod[:, :d], mod[:, d:2 * d], mod[:, 2 * d:]
    rep = lambda a: jnp.repeat(a[:n_seq], n_tok, axis=0)
    q_scale = LOG2E / math.sqrt(SB_HEAD_DIM)

    def in_proj(h, tm, want_16):
        kv_types = (F32, BF16) if want_16 else (F32,)
        (ua,) = _proj(h, w_a, 0, c0, (F32,), tm, 1024, "proj_gla",
                      lead_cols=key_w, lead_scale=dk ** -0.5)
        (glr,) = _proj(h, w_glr, 0, GLA_RANK, (F32,), tm, GLA_RANK, "proj_glr")
        (q16,) = _proj(h, w_b, 0, sb_w, (BF16,), tm, 1024, "proj_qb",
                       norm_gain=q_norm_g, cast_scale=q_scale)
        k_out = _proj(h, w_b, sb_w, sb_w, kv_types, tm, 1024, "proj_kb", norm_gain=k_norm_g)
        v_out = _proj(h, w_b, 2 * sb_w, sb_w, kv_types, tm, 1024, "proj_vb")
        (uc,) = _proj(h, w_b, 3 * sb_w, sb_w + 2 * d, (F32,), tm, 1024, "proj_gates")
        return ua, glr, q16, k_out, v_out, uc

    hp = _mod_norm(xp, norm_g, scale[n_seq:n_seq + 1], shift[n_seq:n_seq + 1], 512)
    ua, glr, q16, (k32, k16), (v32, v16), uc = in_proj(hp, 1024, True)
    oa, s_p = _gla_prompt(ua, glr, wg16, b_gate, gla_norm_g, dk, dv)
    ob = _sb_prompt(q16, k16, v16, uc, sb_bias)
    y = _merge(oa, ob, wa16, wb16, uc, sb_w, sb_w + d, 512, 1024)
    out_p = _out_proj(y, wo16, xp, gate[n_seq:n_seq + 1], 1024, 1024)
    kp, vp = k32, v32

    m_s = xs.shape[0]
    hs = _mod_norm(xs, norm_g, rep(scale), rep(shift), m_s)
    ua, glr, q16, (k32,), (v32,), uc = in_proj(hs, m_s, False)
    ua3 = _pad_rows(ua.reshape(n_seq, n_tok, -1), GLA_PAD_ROWS)
    glr3 = _pad_rows(glr.reshape(n_seq, n_tok, -1), GLA_PAD_ROWS)
    oa3, s_s = _gla_sample(ua3, glr3, wg16, b_gate, gla_norm_g, state, n_tok)
    oa = oa3[:, :n_tok].reshape(m_s, val_w).astype(BF16)

    ngrp = heads // SB_PAIR
    qg = q16.reshape(n_seq, n_tok, ngrp, SB_PAIR, SB_HEAD_DIM).transpose(0, 2, 3, 1, 4)
    eye = jnp.eye(SB_PAIR, dtype=BF16)[None, None, :, :, None, None]
    q8 = (qg[:, :, :, None, :, :] * eye).reshape(n_seq, heads, SB_PAIR * n_tok, SB_HEAD_DIM)
    z_g = (uc[:, :sb_w].reshape(n_seq, n_tok, ngrp, SB_PAIR, SB_HEAD_DIM)
           .transpose(0, 2, 3, 1, 4).reshape(n_seq, ngrp, SB_PAIR * n_tok, SB_HEAD_DIM))
    bias_rows = jnp.broadcast_to(jnp.repeat(sb_bias.astype(F32) * LOG2E, n_tok)[:, None],
                                 (heads * n_tok, SB_HEAD_DIM))
    k_new = k32.reshape(n_seq, n_tok * heads, SB_HEAD_DIM)
    v_new = v32.reshape(n_seq, n_tok * heads, SB_HEAD_DIM)
    ob_g = _sb_sample(page_table, q8, cache_k, cache_v, k_new, v_new, z_g, bias_rows, n_tok)
    ob = (ob_g.reshape(n_seq, ngrp, SB_PAIR, n_tok, SB_HEAD_DIM).transpose(0, 3, 1, 2, 4)
          .reshape(m_s, sb_w).astype(BF16))
    y = _merge(oa, ob, wa16, wb16, uc, sb_w, sb_w + d, m_s, 1024)
    out_s = _out_proj(y, wo16, xs, rep(gate), m_s, 1024)
    return out_p, out_s, kp, vp, s_p, k32, v32, s_s


def kernel(x_prompt, x_sample, cache_k, cache_v, state_gla, page_table, c_prompt, c_sample,
           w_ada, b_ada, norm_g, w_in, w_gate_up, b_gate, gla_norm_g, q_norm_g, k_norm_g,
           sb_bias, w_branch_a, w_branch_b, w_out):
    depth = w_ada.shape[0]
    n_b, t_len, d = x_prompt.shape
    assert n_b == 1, "the prompt group is a single sequence"
    n_seq, n_tok, _ = x_sample.shape
    heads = d // SB_HEAD_DIM
    pad = (-(n_seq + n_b)) % 8
    c_all = jnp.concatenate([c_sample, c_prompt, jnp.zeros((pad, d), F32)], axis=0)
    xp = x_prompt.reshape(t_len, d)
    xs = x_sample.reshape(n_seq * n_tok, d)
    outs = [[] for _ in range(6)]
    for l in range(depth):
        w = (w_ada[l], b_ada[l], norm_g[l], w_in[l], w_gate_up[l], b_gate[l], gla_norm_g[l],
             q_norm_g[l], k_norm_g[l], sb_bias[l], w_branch_a[l], w_branch_b[l], w_out[l])
        xp, xs, kp, vp, sp, ks, vs, ss = _layer(xp, xs, c_all, w, cache_k[l], cache_v[l],
                                                 state_gla[l], page_table, n_seq, n_tok)
        outs[0].append(kp.reshape(n_b, t_len, heads, SB_HEAD_DIM))
        outs[1].append(vp.reshape(n_b, t_len, heads, SB_HEAD_DIM))
        outs[2].append(sp[None])
        outs[3].append(ks.reshape(n_seq, n_tok, heads, SB_HEAD_DIM))
        outs[4].append(vs.reshape(n_seq, n_tok, heads, SB_HEAD_DIM))
        outs[5].append(ss)
    k_p, v_p, s_p, k_s, v_s, s_s = (jnp.stack(o) for o in outs)
    return (xp.reshape(n_b, t_len, d), xs.reshape(n_seq, n_tok, d), k_p, v_p, s_p, k_s, v_s, s_s)
```

```python
import functools
import math

import numpy as np
import jax
import jax.numpy as jnp
from jax import lax
from jax.experimental import pallas as pl
from jax.experimental.pallas import tpu as pltpu

F32 = jnp.float32
BF16 = jnp.bfloat16

GLA_HEADS = 4
GLA_RANK = 16
GLA_TAU = 16.0
SB_HEAD_DIM = 128
NORM_EPS = 1e-6
LOG2E = 1.4426950408889634
LN2 = 0.6931471805599453

LANES = 128
VMEM_LIMIT = 56 * 1024 * 1024


def _cparams(sem):
    return pltpu.CompilerParams(dimension_semantics=sem, vmem_limit_bytes=VMEM_LIMIT)


def _sigmoid(x):
    return 1.0 / (1.0 + jnp.exp(-x))


def _softplus_parts(z):
    l = jnp.log(1.0 + jnp.exp(-jnp.abs(z)))
    return jnp.maximum(z, 0.0) + l, l


def _ada_kernel(c_ref, w_ref, b_ref, o_ref):
    c = c_ref[...]
    s = (c * _sigmoid(c)).astype(BF16)
    o_ref[...] = jnp.dot(s, w_ref[...].astype(BF16), preferred_element_type=F32) + b_ref[...]


def _ada(c_all, w_ada, b_ada, tn=512):
    r, d = c_all.shape
    n = w_ada.shape[1]
    return pl.pallas_call(
        _ada_kernel,
        out_shape=jax.ShapeDtypeStruct((r, n), F32),
        grid=(n // tn,),
        in_specs=[pl.BlockSpec((r, d), lambda j: (0, 0)),
                  pl.BlockSpec((d, tn), lambda j: (0, j)),
                  pl.BlockSpec((1, tn), lambda j: (0, j))],
        out_specs=pl.BlockSpec((r, tn), lambda j: (0, j)),
        compiler_params=_cparams(("arbitrary",)),
        name="ada_mod",
    )(c_all, w_ada, b_ada.reshape(1, n))


def _norm_kernel(x_ref, g_ref, scale_ref, shift_ref, h_ref):
    x = x_ref[...]
    ms = jnp.mean(x * x, axis=-1, keepdims=True)
    y = x * lax.rsqrt(ms + NORM_EPS) * g_ref[...]
    h_ref[...] = (y * (1.0 + scale_ref[...]) + shift_ref[...]).astype(h_ref.dtype)


def _mod_norm(x, g, scale, shift, tm):
    m, d = x.shape
    per_row = scale.shape[0] != 1
    mod_spec = (pl.BlockSpec((tm, d), lambda i: (i, 0)) if per_row
                else pl.BlockSpec((1, d), lambda i: (0, 0)))
    return pl.pallas_call(
        _norm_kernel,
        out_shape=jax.ShapeDtypeStruct((m, d), BF16),
        grid=(m // tm,),
        in_specs=[pl.BlockSpec((tm, d), lambda i: (i, 0)),
                  pl.BlockSpec((1, d), lambda i: (0, 0)),
                  mod_spec, mod_spec],
        out_specs=pl.BlockSpec((tm, d), lambda i: (i, 0)),
        compiler_params=_cparams(("arbitrary",)),
        name="mod_norm",
    )(x, g.reshape(1, d), scale, shift)


def _proj_kernel(h_ref, w_ref, *refs, lead_tiles, lead_scale, head_norm, cast_scale):
    o_refs = refs[1:] if head_norm else refs
    acc = jnp.dot(h_ref[...], w_ref[...], preferred_element_type=F32)
    if lead_tiles:
        acc = acc * jnp.where(pl.program_id(1) < lead_tiles, lead_scale, 1.0)

    def emit(cols, y):
        for o_ref in o_refs:
            o_ref[:, cols] = y if o_ref.dtype == F32 else (y * cast_scale).astype(o_ref.dtype)

    if head_norm:
        g = refs[0][...]
        for c in range(acc.shape[1] // SB_HEAD_DIM):
            cols = slice(c * SB_HEAD_DIM, (c + 1) * SB_HEAD_DIM)
            blk = acc[:, cols]
            ms = jnp.mean(blk * blk, axis=-1, keepdims=True)
            emit(cols, blk * lax.rsqrt(ms + NORM_EPS) * g)
    else:
        emit(slice(None), acc)


def _proj(h, w, col0, n, out_dtypes, tm, tn, name, *, lead_cols=0, lead_scale=1.0,
          norm_gain=None, cast_scale=1.0):
    m, k = h.shape
    tn = min(tn, n)
    j0 = col0 // tn
    assert col0 % tn == 0 and n % tn == 0 and m % tm == 0 and lead_cols % tn == 0
    in_specs = [pl.BlockSpec((tm, k), lambda i, j: (i, 0)),
                pl.BlockSpec((k, tn), lambda i, j: (0, j0 + j))]
    args = [h, w]
    if norm_gain is not None:
        in_specs.append(pl.BlockSpec((1, SB_HEAD_DIM), lambda i, j: (0, 0)))
        args.append(norm_gain.reshape(1, SB_HEAD_DIM))
    tile = pl.BlockSpec((tm, tn), lambda i, j: (i, j))
    return pl.pallas_call(
        functools.partial(_proj_kernel, lead_tiles=lead_cols // tn, lead_scale=lead_scale,
                          head_norm=norm_gain is not None, cast_scale=cast_scale),
        out_shape=tuple(jax.ShapeDtypeStruct((m, n), dt) for dt in out_dtypes),
        grid=(m // tm, n // tn),
        in_specs=in_specs,
        out_specs=tuple(tile for _ in out_dtypes),
        compiler_params=_cparams(("arbitrary", "arbitrary")),
        name=name,
    )(*args)


def _merge_kernel(oa_ref, ob_ref, wa_ref, wb_ref, ga_ref, gb_ref, y_ref):
    ya = jnp.dot(oa_ref[...], wa_ref[...], preferred_element_type=F32)
    yb = jnp.dot(ob_ref[...], wb_ref[...], preferred_element_type=F32)
    y_ref[...] = (_sigmoid(ga_ref[...]) * ya + _sigmoid(gb_ref[...]) * yb).astype(y_ref.dtype)


def _merge(oa, ob, wa, wb, uc, ga_col, gb_col, tm, tn):
    m, k = oa.shape
    n = wa.shape[1]
    row = lambda i, j: (i, 0)
    col = lambda i, j: (0, j)
    ja, jb = ga_col // tn, gb_col // tn
    return pl.pallas_call(
        _merge_kernel,
        out_shape=jax.ShapeDtypeStruct((m, n), BF16),
        grid=(m // tm, n // tn),
        in_specs=[pl.BlockSpec((tm, k), row), pl.BlockSpec((tm, k), row),
                  pl.BlockSpec((k, tn), col), pl.BlockSpec((k, tn), col),
                  pl.BlockSpec((tm, tn), lambda i, j: (i, ja + j)),
                  pl.BlockSpec((tm, tn), lambda i, j: (i, jb + j))],
        out_specs=pl.BlockSpec((tm, tn), lambda i, j: (i, j)),
        compiler_params=_cparams(("arbitrary", "arbitrary")),
        name="merge",
    )(oa, ob, wa, wb, uc, uc)


def _out_kernel(y_ref, w_ref, x_ref, gate_ref, o_ref):
    acc = jnp.dot(y_ref[...], w_ref[...], preferred_element_type=F32)
    o_ref[...] = x_ref[...] + gate_ref[...] * acc


def _out_proj(y, w, x, gate, tm, tn):
    m, k = y.shape
    n = w.shape[1]
    per_row = gate.shape[0] != 1
    gate_spec = (pl.BlockSpec((tm, tn), lambda i, j: (i, j)) if per_row
                 else pl.BlockSpec((1, tn), lambda i, j: (0, j)))
    return pl.pallas_call(
        _out_kernel,
        out_shape=jax.ShapeDtypeStruct((m, n), F32),
        grid=(m // tm, n // tn),
        in_specs=[pl.BlockSpec((tm, k), lambda i, j: (i, 0)),
                  pl.BlockSpec((k, tn), lambda i, j: (0, j)),
                  pl.BlockSpec((tm, tn), lambda i, j: (i, j)),
                  gate_spec],
        out_specs=pl.BlockSpec((tm, tn), lambda i, j: (i, j)),
        compiler_params=_cparams(("arbitrary", "arbitrary")),
        name="out_proj",
    )(y, w, x, gate)


GLA_SUB = 16
GLA_BLK = 64
GLA_ROWS = 256


def _split3(x):
    hi = x.astype(BF16)
    r1 = x - hi.astype(F32)
    mid = r1.astype(BF16)
    lo = (r1 - mid.astype(F32)).astype(BF16)
    return hi, mid, lo


def _gla_prompt_kernel(q_ref, k_ref, v_ref, z_ref, glr_ref, wg_ref, bg_ref, gn_ref, tri_ref,
                       o_ref, s_ref, st_ref, b_ref):
    t = pl.program_id(0)
    nt = pl.num_programs(0)
    rows = q_ref.shape[0]
    nh, dk, dv = st_ref.shape
    c = GLA_SUB
    blk = min(GLA_BLK, rows)

    @pl.when(t == 0)
    def _():
        st_ref[...] = jnp.zeros_like(st_ref)

    x = jnp.dot(glr_ref[...].astype(BF16), wg_ref[...], preferred_element_type=F32) + bg_ref[...]
    la = (jnp.minimum(x, 0.0) - jnp.log(1.0 + jnp.exp(-jnp.abs(x)))) * (1.0 / GLA_TAU)
    tri = tri_ref[...]
    hi, mid, lo = _split3(la)
    b_ref[...] = (jnp.dot(tri, hi, preferred_element_type=F32)
                  + jnp.dot(tri, mid, preferred_element_type=F32)
                  + jnp.dot(tri, lo, preferred_element_type=F32))

    row_id = lax.broadcasted_iota(jnp.int32, (c // 2, 1), 0)
    lane_id = lax.broadcasted_iota(jnp.int32, (c // 2, c), 1)
    gn = gn_ref[...]

    heads = range(nh)
    ksl = [slice(h * dk, (h + 1) * dk) for h in heads]
    vsl = [slice(h * dv, (h + 1) * dv) for h in heads]
    nt_dims = (((1,), (1,)), ((), ()))

    def sub_chunk(r0, i, st16):
        ri = r0 + i * c
        rs = pl.ds(ri, c)
        qc = [q_ref[rs, ksl[h]] for h in heads]
        bc = [b_ref[rs, ksl[h]] for h in heads]
        o_acc = [jnp.dot((qc[h] * jnp.exp(bc[h])).astype(BF16), st16[h],
                         preferred_element_type=F32) for h in heads]
        if i > 0:
            prev = pl.ds(r0, i * c)
            for h in heads:
                b_ref0 = b_ref[pl.ds(ri - 1, 1), ksl[h]]
                qq = (qc[h] * jnp.exp(bc[h] - b_ref0)).astype(BF16)
                kk = (k_ref[prev, ksl[h]] * jnp.exp(b_ref0 - b_ref[prev, ksl[h]])).astype(BF16)
                a_off = lax.dot_general(qq, kk, nt_dims, preferred_element_type=F32)
                o_acc[h] = o_acc[h] + jnp.dot(a_off.astype(BF16), v_ref[prev, vsl[h]].astype(BF16),
                                              preferred_element_type=F32)
        half = c // 2
        a_top = [jnp.zeros((half, c), F32) for _ in heads]
        a_bot = [jnp.zeros((half, c), F32) for _ in heads]
        for s in range(c):
            for h in heads:
                ks = k_ref[pl.ds(ri + s, 1), ksl[h]]
                bs = b_ref[pl.ds(ri + s, 1), ksl[h]]
                if s < half:
                    w = jnp.where(row_id >= s,
                                  qc[h][:half] * ks * jnp.exp(bc[h][:half] - bs), 0.0)
                    a_top[h] = jnp.where(lane_id == s, jnp.sum(w, axis=-1, keepdims=True), a_top[h])
                    w = qc[h][half:] * ks * jnp.exp(bc[h][half:] - bs)
                else:
                    w = jnp.where(row_id >= s - half,
                                  qc[h][half:] * ks * jnp.exp(bc[h][half:] - bs), 0.0)
                a_bot[h] = jnp.where(lane_id == s, jnp.sum(w, axis=-1, keepdims=True), a_bot[h])
        for h in heads:
            a = jnp.concatenate([a_top[h], a_bot[h]], axis=0).astype(BF16)
            o = o_acc[h] + jnp.dot(a, v_ref[rs, vsl[h]].astype(BF16), preferred_element_type=F32)
            ms = jnp.mean(o * o, axis=-1, keepdims=True)
            zc = z_ref[rs, vsl[h]]
            y = o * lax.rsqrt(ms + NORM_EPS) * gn * (zc * _sigmoid(zc))
            o_ref[rs, vsl[h]] = y.astype(o_ref.dtype)

    def block(bi, carry):
        r0 = pl.multiple_of(bi * blk, blk)
        rb = pl.ds(r0, blk)
        st = [st_ref[h] for h in heads]
        st16 = [s.astype(BF16) for s in st]
        for i in range(blk // c):
            sub_chunk(r0, i, st16)
        for h in heads:
            bb = b_ref[rb, ksl[h]]
            kd = (k_ref[rb, ksl[h]] * jnp.exp(bb[blk - 1:blk] - bb)).astype(BF16)
            upd = lax.dot_general(kd, v_ref[rb, vsl[h]].astype(BF16), (((0,), (0,)), ((), ())),
                                  preferred_element_type=F32)
            st_ref[h] = st[h] * jnp.exp(bb[blk - 8:].T[:, 7:8]) + upd
        return carry

    lax.fori_loop(0, rows // blk, block, 0)

    @pl.when(t == nt - 1)
    def _():
        s_ref[...] = st_ref[...]


def _gla_prompt(ua, glr, wg, bg, gn, dk, dv):
    t_len = ua.shape[0]
    h = GLA_HEADS
    rows = min(GLA_ROWS, t_len)
    r = np.arange(rows)
    tri = ((r[:, None] >= r[None, :]) & (r[:, None] // GLA_BLK == r[None, :] // GLA_BLK))
    tri = jnp.asarray(tri, dtype=BF16)
    v_blk = 2 * h * dk // (h * dv)
    return pl.pallas_call(
        _gla_prompt_kernel,
        out_shape=(jax.ShapeDtypeStruct((t_len, h * dv), BF16),
                   jax.ShapeDtypeStruct((h, dk, dv), F32)),
        grid=(t_len // rows,),
        in_specs=[pl.BlockSpec((rows, h * dk), lambda tt: (tt, 0)),
                  pl.BlockSpec((rows, h * dk), lambda tt: (tt, 1)),
                  pl.BlockSpec((rows, h * dv), lambda tt: (tt, v_blk)),
                  pl.BlockSpec((rows, h * dv), lambda tt: (tt, v_blk + 1)),
                  pl.BlockSpec((rows, GLA_RANK), lambda tt: (tt, 0)),
                  pl.BlockSpec((GLA_RANK, h * dk), lambda tt: (0, 0)),
                  pl.BlockSpec((1, h * dk), lambda tt: (0, 0)),
                  pl.BlockSpec((1, dv), lambda tt: (0, 0)),
                  pl.BlockSpec((rows, rows), lambda tt: (0, 0))],
        out_specs=(pl.BlockSpec((rows, h * dv), lambda tt: (tt, 0)),
                   pl.BlockSpec((h, dk, dv), lambda tt: (0, 0, 0))),
        scratch_shapes=[pltpu.VMEM((h, dk, dv), F32), pltpu.VMEM((rows, h * dk), F32)],
        compiler_params=_cparams(("arbitrary",)),
        name="gla_prompt",
    )(ua, ua, ua, ua, glr, wg, bg.reshape(1, -1), gn.reshape(1, -1), tri)


GLA_PAD_ROWS = 8


def _log_decay(x):
    return (jnp.minimum(x, 0.0) - jnp.log(1.0 + jnp.exp(-jnp.abs(x)))) * (1.0 / GLA_TAU)


def _gla_sample_kernel(q_ref, k_ref, v_ref, z_ref, glr_ref, wg_ref, bg_ref, gn_ref, s_ref,
                       o_ref, so_ref, *, n_tok):
    rows = q_ref.shape[1]
    dk = s_ref.shape[2]
    dv = s_ref.shape[3]
    row_id = lax.broadcasted_iota(jnp.int32, (rows, 1), 0)
    gn = gn_ref[...]
    glr = glr_ref[0].astype(BF16)
    heads = range(GLA_HEADS)
    ksl = [slice(h * dk, (h + 1) * dk) for h in heads]
    vsl = [slice(h * dv, (h + 1) * dv) for h in heads]
    la = _log_decay(jnp.dot(glr, wg_ref[...], preferred_element_type=F32) + bg_ref[...])
    b_all = jnp.zeros_like(la)
    for s in range(n_tok):
        b_all = b_all + jnp.where(row_id >= s, la[s:s + 1, :], 0.0)
    bc = [b_all[:, ksl[h]] for h in heads]
    qc = [q_ref[0, :, ksl[h]] for h in heads]
    kc = [k_ref[0, :, ksl[h]] for h in heads]
    vc = [v_ref[0, :, vsl[h]] for h in heads]
    st = [s_ref[0, h] for h in heads]
    o = [jnp.dot((qc[h] * jnp.exp(bc[h])).astype(BF16), st[h].astype(BF16),
                 preferred_element_type=F32) for h in heads]
    upd = [lax.dot_general((kc[h] * jnp.exp(bc[h][n_tok - 1:n_tok] - bc[h])).astype(BF16),
                           vc[h].astype(BF16), (((0,), (0,)), ((), ())),
                           preferred_element_type=F32) for h in heads]
    decay_col = [jnp.exp(bc[h].T[:, n_tok - 1:n_tok]) for h in heads]
    for s in range(n_tok):
        for h in heads:
            w = jnp.where(row_id >= s,
                          qc[h] * kc[h][s:s + 1] * jnp.exp(bc[h] - bc[h][s:s + 1]), 0.0)
            o[h] = o[h] + jnp.sum(w, axis=-1, keepdims=True) * vc[h][s:s + 1]
    for h in heads:
        ms = jnp.mean(o[h] * o[h], axis=-1, keepdims=True)
        zc = z_ref[0, :, vsl[h]]
        o_ref[0, :, vsl[h]] = o[h] * lax.rsqrt(ms + NORM_EPS) * gn * (zc * _sigmoid(zc))
        so_ref[0, h] = st[h] * decay_col[h] + upd[h]


def _gla_sample(ua, glr, wg, bg, gn, state, n_tok):
    b, rows, _ = ua.shape
    h, dk, dv = state.shape[1:]
    k_blk = h * dk // (h * dk)
    v_blk = 2 * h * dk // (h * dv)
    z_blk = v_blk + 1
    return pl.pallas_call(
        functools.partial(_gla_sample_kernel, n_tok=n_tok),
        out_shape=(jax.ShapeDtypeStruct((b, rows, h * dv), F32),
                   jax.ShapeDtypeStruct(state.shape, F32)),
        grid=(b,),
        in_specs=[pl.BlockSpec((1, rows, h * dk), lambda i: (i, 0, 0)),
                  pl.BlockSpec((1, rows, h * dk), lambda i: (i, 0, k_blk)),
                  pl.BlockSpec((1, rows, h * dv), lambda i: (i, 0, v_blk)),
                  pl.BlockSpec((1, rows, h * dv), lambda i: (i, 0, z_blk)),
                  pl.BlockSpec((1, rows, GLA_RANK), lambda i: (i, 0, 0)),
                  pl.BlockSpec((GLA_RANK, h * dk), lambda i: (0, 0)),
                  pl.BlockSpec((1, h * dk), lambda i: (0, 0)),
                  pl.BlockSpec((1, dv), lambda i: (0, 0)),
                  pl.BlockSpec((1, h, dk, dv), lambda i: (i, 0, 0, 0))],
        out_specs=(pl.BlockSpec((1, rows, h * dv), lambda i: (i, 0, 0)),
                   pl.BlockSpec((1, h, dk, dv), lambda i: (i, 0, 0, 0))),
        compiler_params=_cparams(("arbitrary",)),
        name="gla_sample",
    )(ua, ua, ua, ua, glr, wg, bg.reshape(1, -1), gn.reshape(1, -1), state)


SB_BLK = 2048
SB_SUB = 128


def _suffix_matrix():
    j = np.arange(SB_SUB)
    u = (j[:, None] >= j[None, :]).astype(np.float32)
    half = np.concatenate([u, np.ones((SB_SUB, SB_SUB), np.float32)], axis=1)
    return jnp.asarray(np.concatenate([half, half], axis=0), dtype=BF16)


def _sb_suffix(z, uw, mask):
    neg_abs = lax.bitcast_convert_type(
        lax.bitcast_convert_type(z, jnp.int32) | jnp.int32(-2 ** 31), F32)
    sp = jnp.maximum(z, 0.0) + jnp.log(1.0 + jnp.exp2(neg_abs)) * LOG2E
    if mask is not None:
        sp = jnp.where(mask, sp, 0.0)
    hi = sp.astype(BF16)
    lo = (sp - hi.astype(F32)).astype(BF16)
    return jnp.dot(jnp.concatenate([hi, lo], axis=1), uw, preferred_element_type=F32)


def _sb_weights(z, s2, car, mask):
    a = jnp.exp2(z - s2[:, :SB_SUB] - car)
    if mask is not None:
        a = jnp.where(mask, a, 0.0)
    return a, car + s2[:, SB_SUB:]


def _sb_prompt_kernel(qi_ref, kj_ref, q_ref, k_ref, v_ref, z_ref, bias_ref, uw_ref,
                      o_ref, acc_ref, car_ref):
    step = pl.program_id(1)
    qi = qi_ref[step]
    kj = kj_ref[step]
    blk = q_ref.shape[0]
    nsub = blk // SB_SUB
    uw = uw_ref[...]
    bias = bias_ref[0]

    @pl.when(kj == qi)
    def _():
        acc_ref[...] = jnp.zeros_like(acc_ref)
        car_ref[...] = jnp.zeros_like(car_ref)

    def logits(ks, r0):
        return lax.dot_general(q_ref[r0:blk, :], k_ref[ks * SB_SUB:(ks + 1) * SB_SUB, :],
                               (((1,), (1,)), ((), ())), preferred_element_type=F32) + bias

    def causal_mask(shape, masked):
        if not masked:
            return None
        return lax.broadcasted_iota(jnp.int32, shape, 1) < lax.broadcasted_iota(jnp.int32, shape, 0)

    def finish(ks, r0, z, s2, mask):
        a, car = _sb_weights(z, s2, car_ref[r0:blk, :], mask)
        car_ref[r0:blk, :] = car
        acc_ref[r0:blk, :] += jnp.dot(a.astype(BF16), v_ref[ks * SB_SUB:(ks + 1) * SB_SUB, :],
                                      preferred_element_type=F32)

    def sweep(masked):
        tiles = [(ks, ks * SB_SUB if masked else 0) for ks in range(nsub - 1, -1, -1)]
        z = logits(*tiles[0])
        mask = causal_mask(z.shape, masked)
        s2 = _sb_suffix(z, uw, mask)
        for i, (ks, r0) in enumerate(tiles):
            nxt = tiles[i + 1] if i + 1 < len(tiles) else None
            if nxt is not None:
                z_n = logits(*nxt)
                mask_n = causal_mask(z_n.shape, masked)
            finish(ks, r0, z, s2, mask)
            if nxt is not None:
                s2 = _sb_suffix(z_n, uw, mask_n)
                z, mask = z_n, mask_n

    @pl.when(kj == qi)
    def _():
        sweep(True)

    @pl.when(kj != qi)
    def _():
        sweep(False)

    @pl.when(kj == 0)
    def _():
        zz = z_ref[...]
        o_ref[...] = (acc_ref[...] * (zz * _sigmoid(zz))).astype(o_ref.dtype)


def _sb_prompt(q16, k16, v16, uc, bias):
    t_len, w = q16.shape
    heads = w // SB_HEAD_DIM
    blk = min(SB_BLK, t_len)
    nq = t_len // blk
    qi = np.concatenate([np.full(i + 1, i) for i in range(nq)]).astype(np.int32)
    kj = np.concatenate([np.arange(i, -1, -1) for i in range(nq)]).astype(np.int32)
    bias_b = jnp.broadcast_to((bias.astype(F32) * LOG2E)[:, None, None], (heads, 1, SB_HEAD_DIM))
    qmap = lambda h, s, qi_r, kj_r: (qi_r[s], h)
    kmap = lambda h, s, qi_r, kj_r: (kj_r[s], h)
    return pl.pallas_call(
        _sb_prompt_kernel,
        out_shape=jax.ShapeDtypeStruct((t_len, w), BF16),
        grid_spec=pltpu.PrefetchScalarGridSpec(
            num_scalar_prefetch=2,
            grid=(heads, len(qi)),
            in_specs=[pl.BlockSpec((blk, SB_HEAD_DIM), qmap),
                      pl.BlockSpec((blk, SB_HEAD_DIM), kmap),
                      pl.BlockSpec((blk, SB_HEAD_DIM), kmap),
                      pl.BlockSpec((blk, SB_HEAD_DIM), qmap),
                      pl.BlockSpec((1, 1, SB_HEAD_DIM), lambda h, s, qi_r, kj_r: (h, 0, 0)),
                      pl.BlockSpec((2 * SB_SUB, 2 * SB_SUB), lambda h, s, qi_r, kj_r: (0, 0))],
            out_specs=pl.BlockSpec((blk, SB_HEAD_DIM), qmap),
            scratch_shapes=[pltpu.VMEM((blk, SB_HEAD_DIM), F32),
                            pltpu.VMEM((blk, SB_HEAD_DIM), F32)]),
        compiler_params=_cparams(("arbitrary", "arbitrary")),
        name="sb_prompt",
    )(jnp.asarray(qi), jnp.asarray(kj), q16, k16, v16, uc, bias_b, _suffix_matrix())


SB_PAIR = 2
SB_SEQS = 4


def _sb_sample_kernel(pt_ref, q_ref, kc_ref, vc_ref, kn_ref, vn_ref, z_ref, bias_ref, uw_ref,
                      o_ref, kbuf, vbuf, sem, acc_ref, car_ref, kpg_ref, vpg_ref, *, n_tok, n_pages):
    n_seq, heads, rows, _ = q_ref.shape
    nsq = kbuf.shape[1]
    page_len = kbuf.shape[3]
    ngrp = heads // SB_PAIR
    total = (n_seq // nsq) * n_pages
    uw = uw_ref[...]
    bias = bias_ref[...]

    def page_copies(i, slot):
        i = jnp.minimum(i, total - 1)
        b0 = (i // n_pages) * nsq
        p = i % n_pages
        out = []
        for s in range(nsq):
            pg = pt_ref[(b0 + s) * n_pages + (n_pages - 1 - p)]
            for h in range(heads):
                out.append(pltpu.make_async_copy(kc_ref.at[pg, :, h, :], kbuf.at[slot, s, h],
                                                 sem.at[0, slot]))
                out.append(pltpu.make_async_copy(vc_ref.at[pg, :, h, :], vbuf.at[slot, s, h],
                                                 sem.at[1, slot]))
        return out

    def pages(b0, k_head, v_head, masked):
        zs = []
        for s in range(nsq):
            for g in range(ngrp):
                zg = None
                for h in range(g * SB_PAIR, (g + 1) * SB_PAIR):
                    zh = lax.dot_general(q_ref[b0 + s, h], k_head(s, h), (((1,), (1,)), ((), ())),
                                         preferred_element_type=F32)
                    zg = zh if zg is None else zg + zh
                zs.append(zg)
        z = jnp.concatenate(zs, axis=0) + bias
        mask = None
        if masked:
            tok = lax.broadcasted_iota(jnp.int32, z.shape, 0) % n_tok
            mask = lax.broadcasted_iota(jnp.int32, z.shape, 1) < tok
        a, car = _sb_weights(z, _sb_suffix(z, uw, mask), car_ref[...], mask)
        car_ref[...] = car
        a16 = a.astype(BF16)
        for s in range(nsq):
            for h in range(heads):
                r0 = (s * ngrp + h // SB_PAIR) * rows
                acc_ref[s, h] += jnp.dot(a16[r0:r0 + rows, :], v_head(s, h),
                                         preferred_element_type=F32)

    kpg_ref[...] = jnp.zeros_like(kpg_ref)
    vpg_ref[...] = jnp.zeros_like(vpg_ref)
    for c in page_copies(0, 0):
        c.start()

    def step(i, carry):
        slot = i % 2
        b0 = (i // n_pages) * nsq
        p = i % n_pages
        for c in page_copies(i + 1, 1 - slot):
            c.start()

        @pl.when(p == 0)
        def _():
            acc_ref[...] = jnp.zeros_like(acc_ref)
            car_ref[...] = jnp.zeros_like(car_ref)
            for s in range(nsq):
                kpg_ref[s, 0:kn_ref.shape[1], :] = kn_ref[b0 + s]
                vpg_ref[s, 0:vn_ref.shape[1], :] = vn_ref[b0 + s]
            pages(b0,
                  lambda s, h: kpg_ref[s, pl.ds(h, page_len, stride=heads), :].astype(BF16),
                  lambda s, h: vpg_ref[s, pl.ds(h, page_len, stride=heads), :].astype(BF16), True)

        for c in page_copies(i, slot):
            c.wait()
        pages(b0, lambda s, h: kbuf[slot, s, h].astype(BF16),
              lambda s, h: vbuf[slot, s, h].astype(BF16), False)

        @pl.when(p == n_pages - 1)
        def _():
            first = lax.broadcasted_iota(jnp.int32, (rows, SB_HEAD_DIM), 0) < n_tok
            for s in range(nsq):
                for g in range(ngrp):
                    og = jnp.where(first, acc_ref[s, g * SB_PAIR], acc_ref[s, g * SB_PAIR + 1])
                    zz = z_ref[b0 + s, g]
                    o_ref[b0 + s, g] = og * (zz * _sigmoid(zz))
        return carry

    lax.fori_loop(0, total, step, 0)
    for c in page_copies(total, total % 2):
        c.wait()


def _sb_sample(page_table, q8, cache_k, cache_v, k_new, v_new, z_g, bias_rows, n_tok):
    b, heads, rows, _ = q8.shape
    ngrp = heads // SB_PAIR
    n_pages = page_table.shape[1]
    page = cache_k.shape[1]
    nsq = SB_SEQS
    assert b % nsq == 0
    vmem = pl.BlockSpec(memory_space=pltpu.VMEM)
    hbm = pl.BlockSpec(memory_space=pl.ANY)
    return pl.pallas_call(
        functools.partial(_sb_sample_kernel, n_tok=n_tok, n_pages=n_pages),
        out_shape=jax.ShapeDtypeStruct((b, ngrp, rows, SB_HEAD_DIM), F32),
        in_specs=[pl.BlockSpec(memory_space=pltpu.SMEM), vmem, hbm, hbm,
                  vmem, vmem, vmem, vmem, vmem],
        out_specs=vmem,
        scratch_shapes=[pltpu.VMEM((2, nsq, heads, page, SB_HEAD_DIM), F32),
                        pltpu.VMEM((2, nsq, heads, page, SB_HEAD_DIM), F32),
                        pltpu.SemaphoreType.DMA((2, 2)),
                        pltpu.VMEM((nsq, heads, rows, SB_HEAD_DIM), F32),
                        pltpu.VMEM((nsq * ngrp * rows, SB_HEAD_DIM), F32),
                        pltpu.VMEM((nsq, page * heads, SB_HEAD_DIM), F32),
                        pltpu.VMEM((nsq, page * heads, SB_HEAD_DIM), F32)],
        compiler_params=pltpu.CompilerParams(vmem_limit_bytes=VMEM_LIMIT),
        name="sb_sample",
    )(page_table.reshape(-1), q8, cache_k, cache_v, k_new, v_new, z_g,
      jnp.tile(bias_rows, (nsq, 1)), _suffix_matrix())


def _pad_rows(a, rows):
    return jnp.pad(a, ((0, 0), (0, rows - a.shape[1]), (0, 0)))


def _layer(xp, xs, c_all, w, cache_k, cache_v, state, page_table, n_seq, n_tok):
    (w_ada, b_ada, norm_g, w_in, w_gate_up, b_gate, gla_norm_g,
     q_norm_g, k_norm_g, sb_bias, w_branch_a, w_branch_b, w_out) = w
    d = xp.shape[1]
    heads = d // SB_HEAD_DIM
    dv = gla_norm_g.shape[0]
    key_w = w_gate_up.shape[1]
    dk = key_w // GLA_HEADS
    val_w = GLA_HEADS * dv
    sb_w = heads * SB_HEAD_DIM

    c0 = 2 * key_w + 2 * val_w
    c1 = c0 + GLA_RANK
    w_a = w_in[:, :c0].astype(BF16)
    w_glr = w_in[:, c0:c1].astype(BF16)
    w_b = w_in[:, c1:].astype(BF16)
    wa16, wb16, wo16 = (w_branch_a.astype(BF16), w_branch_b.astype(BF16), w_out.astype(BF16))
    wg16 = w_gate_up.astype(BF16)

    mod = _ada(c_all, w_ada, b_ada)
    shift, scale, gate = mod[:, :d], mod[:, d:2 * d], mod[:, 2 * d:]
    rep = lambda a: jnp.repeat(a[:n_seq], n_tok, axis=0)
    q_scale = LOG2E / math.sqrt(SB_HEAD_DIM)

    def in_proj(h, tm, want_16):
        kv_types = (F32, BF16) if want_16 else (F32,)
        (ua,) = _proj(h, w_a, 0, c0, (F32,), tm, 1024, "proj_gla",
                      lead_cols=key_w, lead_scale=dk ** -0.5)
        (glr,) = _proj(h, w_glr, 0, GLA_RANK, (F32,), tm, GLA_RANK, "proj_glr")
        (q16,) = _proj(h, w_b, 0, sb_w, (BF16,), tm, 1024, "proj_qb",
                       norm_gain=q_norm_g, cast_scale=q_scale)
        k_out = _proj(h, w_b, sb_w, sb_w, kv_types, tm, 1024, "proj_kb", norm_gain=k_norm_g)
        v_out = _proj(h, w_b, 2 * sb_w, sb_w, kv_types, tm, 1024, "proj_vb")
        (uc,) = _proj(h, w_b, 3 * sb_w, sb_w + 2 * d, (F32,), tm, 1024, "proj_gates")
        return ua, glr, q16, k_out, v_out, uc

    hp = _mod_norm(xp, norm_g, scale[n_seq:n_seq + 1], shift[n_seq:n_seq + 1], 512)
    ua, glr, q16, (k32, k16), (v32, v16), uc = in_proj(hp, 1024, True)
    oa, s_p = _gla_prompt(ua, glr, wg16, b_gate, gla_norm_g, dk, dv)
    ob = _sb_prompt(q16, k16, v16, uc, sb_bias)
    y = _merge(oa, ob, wa16, wb16, uc, sb_w, sb_w + d, 512, 1024)
    out_p = _out_proj(y, wo16, xp, gate[n_seq:n_seq + 1], 1024, 1024)
    kp, vp = k32, v32

    m_s = xs.shape[0]
    hs = _mod_norm(xs, norm_g, rep(scale), rep(shift), m_s)
    ua, glr, q16, (k32,), (v32,), uc = in_proj(hs, m_s, False)
    ua3 = _pad_rows(ua.reshape(n_seq, n_tok, -1), GLA_PAD_ROWS)
    glr3 = _pad_rows(glr.reshape(n_seq, n_tok, -1), GLA_PAD_ROWS)
    oa3, s_s = _gla_sample(ua3, glr3, wg16, b_gate, gla_norm_g, state, n_tok)
    oa = oa3[:, :n_tok].reshape(m_s, val_w).astype(BF16)

    ngrp = heads // SB_PAIR
    qg = q16.reshape(n_seq, n_tok, ngrp, SB_PAIR, SB_HEAD_DIM).transpose(0, 2, 3, 1, 4)
    eye = jnp.eye(SB_PAIR, dtype=BF16)[None, None, :, :, None, None]
    q8 = (qg[:, :, :, None, :, :] * eye).reshape(n_seq, heads, SB_PAIR * n_tok, SB_HEAD_DIM)
    z_g = (uc[:, :sb_w].reshape(n_seq, n_tok, ngrp, SB_PAIR, SB_HEAD_DIM)
           .transpose(0, 2, 3, 1, 4).reshape(n_seq, ngrp, SB_PAIR * n_tok, SB_HEAD_DIM))
    bias_rows = jnp.broadcast_to(jnp.repeat(sb_bias.astype(F32) * LOG2E, n_tok)[:, None],
                                 (heads * n_tok, SB_HEAD_DIM))
    k_new = k32.reshape(n_seq, n_tok * heads, SB_HEAD_DIM)
    v_new = v32.reshape(n_seq, n_tok * heads, SB_HEAD_DIM)
    ob_g = _sb_sample(page_table, q8, cache_k, cache_v, k_new, v_new, z_g, bias_rows, n_tok)
    ob = (ob_g.reshape(n_seq, ngrp, SB_PAIR, n_tok, SB_HEAD_DIM).transpose(0, 3, 1, 2, 4)
          .reshape(m_s, sb_w).astype(BF16))
    y = _merge(oa, ob, wa16, wb16, uc, sb_w, sb_w + d, m_s, 1024)
    out_s = _out_proj(y, wo16, xs, rep(gate), m_s, 1024)
    return out_p, out_s, kp, vp, s_p, k32, v32, s_s


def kernel(x_prompt, x_sample, cache_k, cache_v, state_gla, page_table, c_prompt, c_sample,
           w_ada, b_ada, norm_g, w_in, w_gate_up, b_gate, gla_norm_g, q_norm_g, k_norm_g,
           sb_bias, w_branch_a, w_branch_b, w_out):
    depth = w_ada.shape[0]
    n_b, t_len, d = x_prompt.shape
    assert n_b == 1, "the prompt group is a single sequence"
    n_seq, n_tok, _ = x_sample.shape
    heads = d // SB_HEAD_DIM
    pad = (-(n_seq + n_b)) % 8
    c_all = jnp.concatenate([c_sample, c_prompt, jnp.zeros((pad, d), F32)], axis=0)
    xp = x_prompt.reshape(t_len, d)
    xs = x_sample.reshape(n_seq * n_tok, d)
    outs = [[] for _ in range(6)]
    for l in range(depth):
        w = (w_ada[l], b_ada[l], norm_g[l], w_in[l], w_gate_up[l], b_gate[l], gla_norm_g[l],
             q_norm_g[l], k_norm_g[l], sb_bias[l], w_branch_a[l], w_branch_b[l], w_out[l])
        xp, xs, kp, vp, sp, ks, vs, ss = _layer(xp, xs, c_all, w, cache_k[l], cache_v[l],
                                                 state_gla[l], page_table, n_seq, n_tok)
        outs[0].append(kp.reshape(n_b, t_len, heads, SB_HEAD_DIM))
        outs[1].append(vp.reshape(n_b, t_len, heads, SB_HEAD_DIM))
        outs[2].append(sp[None])
        outs[3].append(ks.reshape(n_seq, n_tok, heads, SB_HEAD_DIM))
        outs[4].append(vs.reshape(n_seq, n_tok, heads, SB_HEAD_DIM))
        outs[5].append(ss)
    k_p, v_p, s_p, k_s, v_s, s_s = (jnp.stack(o) for o in outs)
    return (xp.reshape(n_b, t_len, d), xs.reshape(n_seq, n_tok, d), k_p, v_p, s_p, k_s, v_s, s_s)
```

```python
import functools
import math

import numpy as np
import jax
import jax.numpy as jnp
from jax import lax
from jax.experimental import pallas as pl
from jax.experimental.pallas import tpu as pltpu

F32 = jnp.float32
BF16 = jnp.bfloat16

GLA_HEADS = 4
GLA_RANK = 16
GLA_TAU = 16.0
SB_HEAD_DIM = 128
NORM_EPS = 1e-6
LOG2E = 1.4426950408889634
LN2 = 0.6931471805599453

LANES = 128
VMEM_LIMIT = 56 * 1024 * 1024


def _cparams(sem):
    return pltpu.CompilerParams(dimension_semantics=sem, vmem_limit_bytes=VMEM_LIMIT)


def _sigmoid(x):
    return 1.0 / (1.0 + jnp.exp(-x))


def _softplus_parts(z):
    l = jnp.log(1.0 + jnp.exp(-jnp.abs(z)))
    return jnp.maximum(z, 0.0) + l, l


def _ada_kernel(c_ref, w_ref, b_ref, o_ref):
    c = c_ref[...]
    s = (c * _sigmoid(c)).astype(BF16)
    o_ref[...] = jnp.dot(s, w_ref[...].astype(BF16), preferred_element_type=F32) + b_ref[...]


def _ada(c_all, w_ada, b_ada, tn=512):
    r, d = c_all.shape
    n = w_ada.shape[1]
    return pl.pallas_call(
        _ada_kernel,
        out_shape=jax.ShapeDtypeStruct((r, n), F32),
        grid=(n // tn,),
        in_specs=[pl.BlockSpec((r, d), lambda j: (0, 0)),
                  pl.BlockSpec((d, tn), lambda j: (0, j)),
                  pl.BlockSpec((1, tn), lambda j: (0, j))],
        out_specs=pl.BlockSpec((r, tn), lambda j: (0, j)),
        compiler_params=_cparams(("arbitrary",)),
        name="ada_mod",
    )(c_all, w_ada, b_ada.reshape(1, n))


def _norm_kernel(x_ref, g_ref, scale_ref, shift_ref, h_ref):
    x = x_ref[...]
    ms = jnp.mean(x * x, axis=-1, keepdims=True)
    y = x * lax.rsqrt(ms + NORM_EPS) * g_ref[...]
    h_ref[...] = (y * (1.0 + scale_ref[...]) + shift_ref[...]).astype(h_ref.dtype)


def _mod_norm(x, g, scale, shift, tm):
    m, d = x.shape
    per_row = scale.shape[0] != 1
    mod_spec = (pl.BlockSpec((tm, d), lambda i: (i, 0)) if per_row
                else pl.BlockSpec((1, d), lambda i: (0, 0)))
    return pl.pallas_call(
        _norm_kernel,
        out_shape=jax.ShapeDtypeStruct((m, d), BF16),
        grid=(m // tm,),
        in_specs=[pl.BlockSpec((tm, d), lambda i: (i, 0)),
                  pl.BlockSpec((1, d), lambda i: (0, 0)),
                  mod_spec, mod_spec],
        out_specs=pl.BlockSpec((tm, d), lambda i: (i, 0)),
        compiler_params=_cparams(("arbitrary",)),
        name="mod_norm",
    )(x, g.reshape(1, d), scale, shift)


def _proj_kernel(h_ref, w_ref, *refs, lead_tiles, lead_scale, head_norm, cast_scale):
    o_refs = refs[1:] if head_norm else refs
    acc = jnp.dot(h_ref[...], w_ref[...], preferred_element_type=F32)
    if lead_tiles:
        acc = acc * jnp.where(pl.program_id(1) < lead_tiles, lead_scale, 1.0)

    def emit(cols, y):
        for o_ref in o_refs:
            o_ref[:, cols] = y if o_ref.dtype == F32 else (y * cast_scale).astype(o_ref.dtype)

    if head_norm:
        g = refs[0][...]
        for c in range(acc.shape[1] // SB_HEAD_DIM):
            cols = slice(c * SB_HEAD_DIM, (c + 1) * SB_HEAD_DIM)
            blk = acc[:, cols]
            ms = jnp.mean(blk * blk, axis=-1, keepdims=True)
            emit(cols, blk * lax.rsqrt(ms + NORM_EPS) * g)
    else:
        emit(slice(None), acc)


def _proj(h, w, col0, n, out_dtypes, tm, tn, name, *, lead_cols=0, lead_scale=1.0,
          norm_gain=None, cast_scale=1.0):
    m, k = h.shape
    tn = min(tn, n)
    j0 = col0 // tn
    assert col0 % tn == 0 and n % tn == 0 and m % tm == 0 and lead_cols % tn == 0
    in_specs = [pl.BlockSpec((tm, k), lambda i, j: (i, 0)),
                pl.BlockSpec((k, tn), lambda i, j: (0, j0 + j))]
    args = [h, w]
    if norm_gain is not None:
        in_specs.append(pl.BlockSpec((1, SB_HEAD_DIM), lambda i, j: (0, 0)))
        args.append(norm_gain.reshape(1, SB_HEAD_DIM))
    tile = pl.BlockSpec((tm, tn), lambda i, j: (i, j))
    return pl.pallas_call(
        functools.partial(_proj_kernel, lead_tiles=lead_cols // tn, lead_scale=lead_scale,
                          head_norm=norm_gain is not None, cast_scale=cast_scale),
        out_shape=tuple(jax.ShapeDtypeStruct((m, n), dt) for dt in out_dtypes),
        grid=(m // tm, n // tn),
        in_specs=in_specs,
        out_specs=tuple(tile for _ in out_dtypes),
        compiler_params=_cparams(("arbitrary", "arbitrary")),
        name=name,
    )(*args)


def _merge_kernel(oa_ref, ob_ref, wa_ref, wb_ref, ga_ref, gb_ref, y_ref):
    ya = jnp.dot(oa_ref[...], wa_ref[...], preferred_element_type=F32)
    yb = jnp.dot(ob_ref[...], wb_ref[...], preferred_element_type=F32)
    y_ref[...] = (_sigmoid(ga_ref[...]) * ya + _sigmoid(gb_ref[...]) * yb).astype(y_ref.dtype)


def _merge(oa, ob, wa, wb, uc, ga_col, gb_col, tm, tn):
    m, k = oa.shape
    n = wa.shape[1]
    row = lambda i, j: (i, 0)
    col = lambda i, j: (0, j)
    ja, jb = ga_col // tn, gb_col // tn
    return pl.pallas_call(
        _merge_kernel,
        out_shape=jax.ShapeDtypeStruct((m, n), BF16),
        grid=(m // tm, n // tn),
        in_specs=[pl.BlockSpec((tm, k), row), pl.BlockSpec((tm, k), row),
                  pl.BlockSpec((k, tn), col), pl.BlockSpec((k, tn), col),
                  pl.BlockSpec((tm, tn), lambda i, j: (i, ja + j)),
                  pl.BlockSpec((tm, tn), lambda i, j: (i, jb + j))],
        out_specs=pl.BlockSpec((tm, tn), lambda i, j: (i, j)),
        compiler_params=_cparams(("arbitrary", "arbitrary")),
        name="merge",
    )(oa, ob, wa, wb, uc, uc)


def _out_kernel(y_ref, w_ref, x_ref, gate_ref, o_ref):
    acc = jnp.dot(y_ref[...], w_ref[...], preferred_element_type=F32)
    o_ref[...] = x_ref[...] + gate_ref[...] * acc


def _out_proj(y, w, x, gate, tm, tn):
    m, k = y.shape
    n = w.shape[1]
    per_row = gate.shape[0] != 1
    gate_spec = (pl.BlockSpec((tm, tn), lambda i, j: (i, j)) if per_row
                 else pl.BlockSpec((1, tn), lambda i, j: (0, j)))
    return pl.pallas_call(
        _out_kernel,
        out_shape=jax.ShapeDtypeStruct((m, n), F32),
        grid=(m // tm, n // tn),
        in_specs=[pl.BlockSpec((tm, k), lambda i, j: (i, 0)),
                  pl.BlockSpec((k, tn), lambda i, j: (0, j)),
                  pl.BlockSpec((tm, tn), lambda i, j: (i, j)),
                  gate_spec],
        out_specs=pl.BlockSpec((tm, tn), lambda i, j: (i, j)),
        compiler_params=_cparams(("arbitrary", "arbitrary")),
        name="out_proj",
    )(y, w, x, gate)


GLA_SUB = 16
GLA_BLK = 64
GLA_ROWS = 256


def _split3(x):
    hi = x.astype(BF16)
    r1 = x - hi.astype(F32)
    mid = r1.astype(BF16)
    lo = (r1 - mid.astype(F32)).astype(BF16)
    return hi, mid, lo


def _gla_prompt_kernel(q_ref, k_ref, v_ref, z_ref, glr_ref, wg_ref, bg_ref, gn_ref, tri_ref,
                       o_ref, s_ref, st_ref, b_ref):
    t = pl.program_id(0)
    nt = pl.num_programs(0)
    rows = q_ref.shape[0]
    nh, dk, dv = st_ref.shape
    c = GLA_SUB
    blk = min(GLA_BLK, rows)

    @pl.when(t == 0)
    def _():
        st_ref[...] = jnp.zeros_like(st_ref)

    x = jnp.dot(glr_ref[...].astype(BF16), wg_ref[...], preferred_element_type=F32) + bg_ref[...]
    la = (jnp.minimum(x, 0.0) - jnp.log(1.0 + jnp.exp(-jnp.abs(x)))) * (1.0 / GLA_TAU)
    tri = tri_ref[...]
    hi, mid, lo = _split3(la)
    b_ref[...] = (jnp.dot(tri, hi, preferred_element_type=F32)
                  + jnp.dot(tri, mid, preferred_element_type=F32)
                  + jnp.dot(tri, lo, preferred_element_type=F32))

    row_id = lax.broadcasted_iota(jnp.int32, (c // 2, 1), 0)
    lane_id = lax.broadcasted_iota(jnp.int32, (c // 2, c), 1)
    gn = gn_ref[...]

    heads = range(nh)
    ksl = [slice(h * dk, (h + 1) * dk) for h in heads]
    vsl = [slice(h * dv, (h + 1) * dv) for h in heads]
    nt_dims = (((1,), (1,)), ((), ()))

    def sub_chunk(r0, i, st16):
        ri = r0 + i * c
        rs = pl.ds(ri, c)
        qc = [q_ref[rs, ksl[h]] for h in heads]
        bc = [b_ref[rs, ksl[h]] for h in heads]
        o_acc = [jnp.dot((qc[h] * jnp.exp(bc[h])).astype(BF16), st16[h],
                         preferred_element_type=F32) for h in heads]
        if i > 0:
            prev = pl.ds(r0, i * c)
            for h in heads:
                b_ref0 = b_ref[pl.ds(ri - 1, 1), ksl[h]]
                qq = (qc[h] * jnp.exp(bc[h] - b_ref0)).astype(BF16)
                kk = (k_ref[prev, ksl[h]] * jnp.exp(b_ref0 - b_ref[prev, ksl[h]])).astype(BF16)
                a_off = lax.dot_general(qq, kk, nt_dims, preferred_element_type=F32)
                o_acc[h] = o_acc[h] + jnp.dot(a_off.astype(BF16), v_ref[prev, vsl[h]].astype(BF16),
                                              preferred_element_type=F32)
        half = c // 2
        a_top = [jnp.zeros((half, c), F32) for _ in heads]
        a_bot = [jnp.zeros((half, c), F32) for _ in heads]
        for s in range(c):
            for h in heads:
                ks = k_ref[pl.ds(ri + s, 1), ksl[h]]
                bs = b_ref[pl.ds(ri + s, 1), ksl[h]]
                if s < half:
                    w = jnp.where(row_id >= s,
                                  qc[h][:half] * ks * jnp.exp(bc[h][:half] - bs), 0.0)
                    a_top[h] = jnp.where(lane_id == s, jnp.sum(w, axis=-1, keepdims=True), a_top[h])
                    w = qc[h][half:] * ks * jnp.exp(bc[h][half:] - bs)
                else:
                    w = jnp.where(row_id >= s - half,
                                  qc[h][half:] * ks * jnp.exp(bc[h][half:] - bs), 0.0)
                a_bot[h] = jnp.where(lane_id == s, jnp.sum(w, axis=-1, keepdims=True), a_bot[h])
        for h in heads:
            a = jnp.concatenate([a_top[h], a_bot[h]], axis=0).astype(BF16)
            o = o_acc[h] + jnp.dot(a, v_ref[rs, vsl[h]].astype(BF16), preferred_element_type=F32)
            ms = jnp.mean(o * o, axis=-1, keepdims=True)
            zc = z_ref[rs, vsl[h]]
            y = o * lax.rsqrt(ms + NORM_EPS) * gn * (zc * _sigmoid(zc))
            o_ref[rs, vsl[h]] = y.astype(o_ref.dtype)

    def block(bi, carry):
        r0 = pl.multiple_of(bi * blk, blk)
        rb = pl.ds(r0, blk)
        st = [st_ref[h] for h in heads]
        st16 = [s.astype(BF16) for s in st]
        for i in range(blk // c):
            sub_chunk(r0, i, st16)
        for h in heads:
            bb = b_ref[rb, ksl[h]]
            kd = (k_ref[rb, ksl[h]] * jnp.exp(bb[blk - 1:blk] - bb)).astype(BF16)
            upd = lax.dot_general(kd, v_ref[rb, vsl[h]].astype(BF16), (((0,), (0,)), ((), ())),
                                  preferred_element_type=F32)
            st_ref[h] = st[h] * jnp.exp(bb[blk - 8:].T[:, 7:8]) + upd
        return carry

    lax.fori_loop(0, rows // blk, block, 0)

    @pl.when(t == nt - 1)
    def _():
        s_ref[...] = st_ref[...]


def _gla_prompt(ua, glr, wg, bg, gn, dk, dv):
    t_len = ua.shape[0]
    h = GLA_HEADS
    rows = min(GLA_ROWS, t_len)
    r = np.arange(rows)
    tri = ((r[:, None] >= r[None, :]) & (r[:, None] // GLA_BLK == r[None, :] // GLA_BLK))
    tri = jnp.asarray(tri, dtype=BF16)
    v_blk = 2 * h * dk // (h * dv)
    return pl.pallas_call(
        _gla_prompt_kernel,
        out_shape=(jax.ShapeDtypeStruct((t_len, h * dv), BF16),
                   jax.ShapeDtypeStruct((h, dk, dv), F32)),
        grid=(t_len // rows,),
        in_specs=[pl.BlockSpec((rows, h * dk), lambda tt: (tt, 0)),
                  pl.BlockSpec((rows, h * dk), lambda tt: (tt, 1)),
                  pl.BlockSpec((rows, h * dv), lambda tt: (tt, v_blk)),
                  pl.BlockSpec((rows, h * dv), lambda tt: (tt, v_blk + 1)),
                  pl.BlockSpec((rows, GLA_RANK), lambda tt: (tt, 0)),
                  pl.BlockSpec((GLA_RANK, h * dk), lambda tt: (0, 0)),
                  pl.BlockSpec((1, h * dk), lambda tt: (0, 0)),
                  pl.BlockSpec((1, dv), lambda tt: (0, 0)),
                  pl.BlockSpec((rows, rows), lambda tt: (0, 0))],
        out_specs=(pl.BlockSpec((rows, h * dv), lambda tt: (tt, 0)),
                   pl.BlockSpec((h, dk, dv), lambda tt: (0, 0, 0))),
        scratch_shapes=[pltpu.VMEM((h, dk, dv), F32), pltpu.VMEM((rows, h * dk), F32)],
        compiler_params=_cparams(("arbitrary",)),
        name="gla_prompt",
    )(ua, ua, ua, ua, glr, wg, bg.reshape(1, -1), gn.reshape(1, -1), tri)


GLA_PAD_ROWS = 8


def _log_decay(x):
    return (jnp.minimum(x, 0.0) - jnp.log(1.0 + jnp.exp(-jnp.abs(x)))) * (1.0 / GLA_TAU)


def _gla_sample_kernel(q_ref, k_ref, v_ref, z_ref, glr_ref, wg_ref, bg_ref, gn_ref, s_ref,
                       o_ref, so_ref, *, n_tok):
    rows = q_ref.shape[1]
    dk = s_ref.shape[2]
    dv = s_ref.shape[3]
    row_id = lax.broadcasted_iota(jnp.int32, (rows, 1), 0)
    gn = gn_ref[...]
    glr = glr_ref[0].astype(BF16)
    heads = range(GLA_HEADS)
    ksl = [slice(h * dk, (h + 1) * dk) for h in heads]
    vsl = [slice(h * dv, (h + 1) * dv) for h in heads]
    la = _log_decay(jnp.dot(glr, wg_ref[...], preferred_element_type=F32) + bg_ref[...])
    b_all = jnp.zeros_like(la)
    for s in range(n_tok):
        b_all = b_all + jnp.where(row_id >= s, la[s:s + 1, :], 0.0)
    bc = [b_all[:, ksl[h]] for h in heads]
    qc = [q_ref[0, :, ksl[h]] for h in heads]
    kc = [k_ref[0, :, ksl[h]] for h in heads]
    vc = [v_ref[0, :, vsl[h]] for h in heads]
    st = [s_ref[0, h] for h in heads]
    o = [jnp.dot((qc[h] * jnp.exp(bc[h])).astype(BF16), st[h].astype(BF16),
                 preferred_element_type=F32) for h in heads]
    upd = [lax.dot_general((kc[h] * jnp.exp(bc[h][n_tok - 1:n_tok] - bc[h])).astype(BF16),
                           vc[h].astype(BF16), (((0,), (0,)), ((), ())),
                           preferred_element_type=F32) for h in heads]
    decay_col = [jnp.exp(bc[h].T[:, n_tok - 1:n_tok]) for h in heads]
    for s in range(n_tok):
        for h in heads:
            w = jnp.where(row_id >= s,
                          qc[h] * kc[h][s:s + 1] * jnp.exp(bc[h] - bc[h][s:s + 1]), 0.0)
            o[h] = o[h] + jnp.sum(w, axis=-1, keepdims=True) * vc[h][s:s + 1]
    for h in heads:
        ms = jnp.mean(o[h] * o[h], axis=-1, keepdims=True)
        zc = z_ref[0, :, vsl[h]]
        o_ref[0, :, vsl[h]] = o[h] * lax.rsqrt(ms + NORM_EPS) * gn * (zc * _sigmoid(zc))
        so_ref[0, h] = st[h] * decay_col[h] + upd[h]


def _gla_sample(ua, glr, wg, bg, gn, state, n_tok):
    b, rows, _ = ua.shape
    h, dk, dv = state.shape[1:]
    k_blk = h * dk // (h * dk)
    v_blk = 2 * h * dk // (h * dv)
    z_blk = v_blk + 1
    return pl.pallas_call(
        functools.partial(_gla_sample_kernel, n_tok=n_tok),
        out_shape=(jax.ShapeDtypeStruct((b, rows, h * dv), F32),
                   jax.ShapeDtypeStruct(state.shape, F32)),
        grid=(b,),
        in_specs=[pl.BlockSpec((1, rows, h * dk), lambda i: (i, 0, 0)),
                  pl.BlockSpec((1, rows, h * dk), lambda i: (i, 0, k_blk)),
                  pl.BlockSpec((1, rows, h * dv), lambda i: (i, 0, v_blk)),
                  pl.BlockSpec((1, rows, h * dv), lambda i: (i, 0, z_blk)),
                  pl.BlockSpec((1, rows, GLA_RANK), lambda i: (i, 0, 0)),
                  pl.BlockSpec((GLA_RANK, h * dk), lambda i: (0, 0)),
                  pl.BlockSpec((1, h * dk), lambda i: (0, 0)),
                  pl.BlockSpec((1, dv), lambda i: (0, 0)),
                  pl.BlockSpec((1, h, dk, dv), lambda i: (i, 0, 0, 0))],
        out_specs=(pl.BlockSpec((1, rows, h * dv), lambda i: (i, 0, 0)),
                   pl.BlockSpec((1, h, dk, dv), lambda i: (i, 0, 0, 0))),
        compiler_params=_cparams(("arbitrary",)),
        name="gla_sample",
    )(ua, ua, ua, ua, glr, wg, bg.reshape(1, -1), gn.reshape(1, -1), state)


SB_BLK = 2048
SB_SUB = 128


def _suffix_matrix():
    j = np.arange(SB_SUB)
    u = (j[:, None] >= j[None, :]).astype(np.float32)
    half = np.concatenate([u, np.ones((SB_SUB, SB_SUB), np.float32)], axis=1)
    return jnp.asarray(np.concatenate([half, half], axis=0), dtype=BF16)


def _sb_suffix(z, uw, mask):
    neg_abs = lax.bitcast_convert_type(
        lax.bitcast_convert_type(z, jnp.int32) | jnp.int32(-2 ** 31), F32)
    sp = jnp.maximum(z, 0.0) + jnp.log(1.0 + jnp.exp2(neg_abs)) * LOG2E
    if mask is not None:
        sp = jnp.where(mask, sp, 0.0)
    hi = sp.astype(BF16)
    lo = (sp - hi.astype(F32)).astype(BF16)
    return jnp.dot(jnp.concatenate([hi, lo], axis=1), uw, preferred_element_type=F32)


def _sb_weights(z, s2, car, mask):
    a = jnp.exp2(z - s2[:, :SB_SUB] - car)
    if mask is not None:
        a = jnp.where(mask, a, 0.0)
    return a, car + s2[:, SB_SUB:]


def _sb_prompt_kernel(qi_ref, kj_ref, q_ref, k_ref, v_ref, z_ref, bias_ref, uw_ref,
                      o_ref, acc_ref, car_ref):
    step = pl.program_id(1)
    qi = qi_ref[step]
    kj = kj_ref[step]
    blk = q_ref.shape[0]
    nsub = blk // SB_SUB
    uw = uw_ref[...]
    bias = bias_ref[0]

    @pl.when(kj == qi)
    def _():
        acc_ref[...] = jnp.zeros_like(acc_ref)
        car_ref[...] = jnp.zeros_like(car_ref)

    def logits(ks, r0):
        return lax.dot_general(q_ref[r0:blk, :], k_ref[ks * SB_SUB:(ks + 1) * SB_SUB, :],
                               (((1,), (1,)), ((), ())), preferred_element_type=F32) + bias

    def causal_mask(shape, masked):
        if not masked:
            return None
        return lax.broadcasted_iota(jnp.int32, shape, 1) < lax.broadcasted_iota(jnp.int32, shape, 0)

    def finish(ks, r0, z, s2, mask):
        a, car = _sb_weights(z, s2, car_ref[r0:blk, :], mask)
        car_ref[r0:blk, :] = car
        acc_ref[r0:blk, :] += jnp.dot(a.astype(BF16), v_ref[ks * SB_SUB:(ks + 1) * SB_SUB, :],
                                      preferred_element_type=F32)

    def sweep(masked):
        tiles = [(ks, ks * SB_SUB if masked else 0) for ks in range(nsub - 1, -1, -1)]
        z = logits(*tiles[0])
        mask = causal_mask(z.shape, masked)
        s2 = _sb_suffix(z, uw, mask)
        for i, (ks, r0) in enumerate(tiles):
            nxt = tiles[i + 1] if i + 1 < len(tiles) else None
            if nxt is not None:
                z_n = logits(*nxt)
                mask_n = causal_mask(z_n.shape, masked)
            finish(ks, r0, z, s2, mask)
            if nxt is not None:
                s2 = _sb_suffix(z_n, uw, mask_n)
                z, mask = z_n, mask_n

    @pl.when(kj == qi)
    def _():
        sweep(True)

    @pl.when(kj != qi)
    def _():
        sweep(False)

    @pl.when(kj == 0)
    def _():
        zz = z_ref[...]
        o_ref[...] = (acc_ref[...] * (zz * _sigmoid(zz))).astype(o_ref.dtype)


def _sb_prompt(q16, k16, v16, uc, bias):
    t_len, w = q16.shape
    heads = w // SB_HEAD_DIM
    blk = min(SB_BLK, t_len)
    nq = t_len // blk
    qi = np.concatenate([np.full(i + 1, i) for i in range(nq)]).astype(np.int32)
    kj = np.concatenate([np.arange(i, -1, -1) for i in range(nq)]).astype(np.int32)
    bias_b = jnp.broadcast_to((bias.astype(F32) * LOG2E)[:, None, None], (heads, 1, SB_HEAD_DIM))
    qmap = lambda h, s, qi_r, kj_r: (qi_r[s], h)
    kmap = lambda h, s, qi_r, kj_r: (kj_r[s], h)
    return pl.pallas_call(
        _sb_prompt_kernel,
        out_shape=jax.ShapeDtypeStruct((t_len, w), BF16),
        grid_spec=pltpu.PrefetchScalarGridSpec(
            num_scalar_prefetch=2,
            grid=(heads, len(qi)),
            in_specs=[pl.BlockSpec((blk, SB_HEAD_DIM), qmap),
                      pl.BlockSpec((blk, SB_HEAD_DIM), kmap),
                      pl.BlockSpec((blk, SB_HEAD_DIM), kmap),
                      pl.BlockSpec((blk, SB_HEAD_DIM), qmap),
                      pl.BlockSpec((1, 1, SB_HEAD_DIM), lambda h, s, qi_r, kj_r: (h, 0, 0)),
                      pl.BlockSpec((2 * SB_SUB, 2 * SB_SUB), lambda h, s, qi_r, kj_r: (0, 0))],
            out_specs=pl.BlockSpec((blk, SB_HEAD_DIM), qmap),
            scratch_shapes=[pltpu.VMEM((blk, SB_HEAD_DIM), F32),
                            pltpu.VMEM((blk, SB_HEAD_DIM), F32)]),
        compiler_params=_cparams(("arbitrary", "arbitrary")),
        name="sb_prompt",
    )(jnp.asarray(qi), jnp.asarray(kj), q16, k16, v16, uc, bias_b, _suffix_matrix())


SB_PAIR = 2
SB_SEQS = 4


def _sb_sample_kernel(pt_ref, q_ref, kc_ref, vc_ref, kn_ref, vn_ref, z_ref, bias_ref, uw_ref,
                      o_ref, kbuf, vbuf, sem, acc_ref, car_ref, kpg_ref, vpg_ref, *, n_tok, n_pages):
    n_seq, heads, rows, _ = q_ref.shape
    nsq = kbuf.shape[1]
    page_len = kbuf.shape[3]
    ngrp = heads // SB_PAIR
    total = (n_seq // nsq) * n_pages
    uw = uw_ref[...]
    bias = bias_ref[...]

    def page_copies(i, slot):
        i = jnp.minimum(i, total - 1)
        b0 = (i // n_pages) * nsq
        p = i % n_pages
        out = []
        for s in range(nsq):
            pg = pt_ref[(b0 + s) * n_pages + (n_pages - 1 - p)]
            for h in range(heads):
                out.append(pltpu.make_async_copy(kc_ref.at[pg, :, h, :], kbuf.at[slot, s, h],
                                                 sem.at[0, slot]))
                out.append(pltpu.make_async_copy(vc_ref.at[pg, :, h, :], vbuf.at[slot, s, h],
                                                 sem.at[1, slot]))
        return out

    def pages(b0, k_head, v_head, masked):
        zs = []
        for s in range(nsq):
            for g in range(ngrp):
                zg = None
                for h in range(g * SB_PAIR, (g + 1) * SB_PAIR):
                    zh = lax.dot_general(q_ref[b0 + s, h], k_head(s, h), (((1,), (1,)), ((), ())),
                                         preferred_element_type=F32)
                    zg = zh if zg is None else zg + zh
                zs.append(zg)
        z = jnp.concatenate(zs, axis=0) + bias
        mask = None
        if masked:
            tok = lax.broadcasted_iota(jnp.int32, z.shape, 0) % n_tok
            mask = lax.broadcasted_iota(jnp.int32, z.shape, 1) < tok
        a, car = _sb_weights(z, _sb_suffix(z, uw, mask), car_ref[...], mask)
        car_ref[...] = car
        a16 = a.astype(BF16)
        for s in range(nsq):
            for h in range(heads):
                r0 = (s * ngrp + h // SB_PAIR) * rows
                acc_ref[s, h] += jnp.dot(a16[r0:r0 + rows, :], v_head(s, h),
                                         preferred_element_type=F32)

    kpg_ref[...] = jnp.zeros_like(kpg_ref)
    vpg_ref[...] = jnp.zeros_like(vpg_ref)
    def start_all(copies):
        for n, c in enumerate(copies):
            c.start(priority=n % 2)

    start_all(page_copies(0, 0))

    def step(i, carry):
        slot = i % 2
        b0 = (i // n_pages) * nsq
        p = i % n_pages
        start_all(page_copies(i + 1, 1 - slot))

        @pl.when(p == 0)
        def _():
            acc_ref[...] = jnp.zeros_like(acc_ref)
            car_ref[...] = jnp.zeros_like(car_ref)
            for s in range(nsq):
                kpg_ref[s, 0:kn_ref.shape[1], :] = kn_ref[b0 + s]
                vpg_ref[s, 0:vn_ref.shape[1], :] = vn_ref[b0 + s]
            pages(b0,
                  lambda s, h: kpg_ref[s, pl.ds(h, page_len, stride=heads), :].astype(BF16),
                  lambda s, h: vpg_ref[s, pl.ds(h, page_len, stride=heads), :].astype(BF16), True)

        for c in page_copies(i, slot):
            c.wait()
        pages(b0, lambda s, h: kbuf[slot, s, h].astype(BF16),
              lambda s, h: vbuf[slot, s, h].astype(BF16), False)

        @pl.when(p == n_pages - 1)
        def _():
            first = lax.broadcasted_iota(jnp.int32, (rows, SB_HEAD_DIM), 0) < n_tok
            for s in range(nsq):
                for g in range(ngrp):
                    og = jnp.where(first, acc_ref[s, g * SB_PAIR], acc_ref[s, g * SB_PAIR + 1])
                    zz = z_ref[b0 + s, g]
                    o_ref[b0 + s, g] = og * (zz * _sigmoid(zz))
        return carry

    lax.fori_loop(0, total, step, 0)
    for c in page_copies(total, total % 2):
        c.wait()


def _sb_sample(page_table, q8, cache_k, cache_v, k_new, v_new, z_g, bias_rows, n_tok):
    b, heads, rows, _ = q8.shape
    ngrp = heads // SB_PAIR
    n_pages = page_table.shape[1]
    page = cache_k.shape[1]
    nsq = SB_SEQS
    assert b % nsq == 0
    vmem = pl.BlockSpec(memory_space=pltpu.VMEM)
    hbm = pl.BlockSpec(memory_space=pl.ANY)
    return pl.pallas_call(
        functools.partial(_sb_sample_kernel, n_tok=n_tok, n_pages=n_pages),
        out_shape=jax.ShapeDtypeStruct((b, ngrp, rows, SB_HEAD_DIM), F32),
        in_specs=[pl.BlockSpec(memory_space=pltpu.SMEM), vmem, hbm, hbm,
                  vmem, vmem, vmem, vmem, vmem],
        out_specs=vmem,
        scratch_shapes=[pltpu.VMEM((2, nsq, heads, page, SB_HEAD_DIM), F32),
                        pltpu.VMEM((2, nsq, heads, page, SB_HEAD_DIM), F32),
                        pltpu.SemaphoreType.DMA((2, 2)),
                        pltpu.VMEM((nsq, heads, rows, SB_HEAD_DIM), F32),
                        pltpu.VMEM((nsq * ngrp * rows, SB_HEAD_DIM), F32),
                        pltpu.VMEM((nsq, page * heads, SB_HEAD_DIM), F32),
                        pltpu.VMEM((nsq, page * heads, SB_HEAD_DIM), F32)],
        compiler_params=pltpu.CompilerParams(vmem_limit_bytes=VMEM_LIMIT),
        name="sb_sample",
    )(page_table.reshape(-1), q8, cache_k, cache_v, k_new, v_new, z_g,
      jnp.tile(bias_rows, (nsq, 1)), _suffix_matrix())


def _pad_rows(a, rows):
    return jnp.pad(a, ((0, 0), (0, rows - a.shape[1]), (0, 0)))


def _layer(xp, xs, c_all, w, cache_k, cache_v, state, page_table, n_seq, n_tok):
    (w_ada, b_ada, norm_g, w_in, w_gate_up, b_gate, gla_norm_g,
     q_norm_g, k_norm_g, sb_bias, w_branch_a, w_branch_b, w_out) = w
    d = xp.shape[1]
    heads = d // SB_HEAD_DIM
    dv = gla_norm_g.shape[0]
    key_w = w_gate_up.shape[1]
    dk = key_w // GLA_HEADS
    val_w = GLA_HEADS * dv
    sb_w = heads * SB_HEAD_DIM

    c0 = 2 * key_w + 2 * val_w
    c1 = c0 + GLA_RANK
    w_a = w_in[:, :c0].astype(BF16)
    w_glr = w_in[:, c0:c1].astype(BF16)
    w_b = w_in[:, c1:].astype(BF16)
    wa16, wb16, wo16 = (w_branch_a.astype(BF16), w_branch_b.astype(BF16), w_out.astype(BF16))
    wg16 = w_gate_up.astype(BF16)

    mod = _ada(c_all, w_ada, b_ada)
    shift, scale, gate = mod[:, :d], mod[:, d:2 * d], mod[:, 2 * d:]
    rep = lambda a: jnp.repeat(a[:n_seq], n_tok, axis=0)
    q_scale = LOG2E / math.sqrt(SB_HEAD_DIM)

    def in_proj(h, tm, want_16):
        kv_types = (F32, BF16) if want_16 else (F32,)
        (ua,) = _proj(h, w_a, 0, c0, (F32,), tm, 1024, "proj_gla",
                      lead_cols=key_w, lead_scale=dk ** -0.5)
        (glr,) = _proj(h, w_glr, 0, GLA_RANK, (F32,), tm, GLA_RANK, "proj_glr")
        (q16,) = _proj(h, w_b, 0, sb_w, (BF16,), tm, 1024, "proj_qb",
                       norm_gain=q_norm_g, cast_scale=q_scale)
        k_out = _proj(h, w_b, sb_w, sb_w, kv_types, tm, 1024, "proj_kb", norm_gain=k_norm_g)
        v_out = _proj(h, w_b, 2 * sb_w, sb_w, kv_types, tm, 1024, "proj_vb")
        (uc,) = _proj(h, w_b, 3 * sb_w, sb_w + 2 * d, (F32,), tm, 1024, "proj_gates")
        return ua, glr, q16, k_out, v_out, uc

    hp = _mod_norm(xp, norm_g, scale[n_seq:n_seq + 1], shift[n_seq:n_seq + 1], 512)
    ua, glr, q16, (k32, k16), (v32, v16), uc = in_proj(hp, 1024, True)
    oa, s_p = _gla_prompt(ua, glr, wg16, b_gate, gla_norm_g, dk, dv)
    ob = _sb_prompt(q16, k16, v16, uc, sb_bias)
    y = _merge(oa, ob, wa16, wb16, uc, sb_w, sb_w + d, 512, 1024)
    out_p = _out_proj(y, wo16, xp, gate[n_seq:n_seq + 1], 1024, 1024)
    kp, vp = k32, v32

    m_s = xs.shape[0]
    hs = _mod_norm(xs, norm_g, rep(scale), rep(shift), m_s)
    ua, glr, q16, (k32,), (v32,), uc = in_proj(hs, m_s, False)
    ua3 = _pad_rows(ua.reshape(n_seq, n_tok, -1), GLA_PAD_ROWS)
    glr3 = _pad_rows(glr.reshape(n_seq, n_tok, -1), GLA_PAD_ROWS)
    oa3, s_s = _gla_sample(ua3, glr3, wg16, b_gate, gla_norm_g, state, n_tok)
    oa = oa3[:, :n_tok].reshape(m_s, val_w).astype(BF16)

    ngrp = heads // SB_PAIR
    qg = q16.reshape(n_seq, n_tok, ngrp, SB_PAIR, SB_HEAD_DIM).transpose(0, 2, 3, 1, 4)
    eye = jnp.eye(SB_PAIR, dtype=BF16)[None, None, :, :, None, None]
    q8 = (qg[:, :, :, None, :, :] * eye).reshape(n_seq, heads, SB_PAIR * n_tok, SB_HEAD_DIM)
    z_g = (uc[:, :sb_w].reshape(n_seq, n_tok, ngrp, SB_PAIR, SB_HEAD_DIM)
           .transpose(0, 2, 3, 1, 4).reshape(n_seq, ngrp, SB_PAIR * n_tok, SB_HEAD_DIM))
    bias_rows = jnp.broadcast_to(jnp.repeat(sb_bias.astype(F32) * LOG2E, n_tok)[:, None],
                                 (heads * n_tok, SB_HEAD_DIM))
    k_new = k32.reshape(n_seq, n_tok * heads, SB_HEAD_DIM)
    v_new = v32.reshape(n_seq, n_tok * heads, SB_HEAD_DIM)
    ob_g = _sb_sample(page_table, q8, cache_k, cache_v, k_new, v_new, z_g, bias_rows, n_tok)
    ob = (ob_g.reshape(n_seq, ngrp, SB_PAIR, n_tok, SB_HEAD_DIM).transpose(0, 3, 1, 2, 4)
          .reshape(m_s, sb_w).astype(BF16))
    y = _merge(oa, ob, wa16, wb16, uc, sb_w, sb_w + d, m_s, 1024)
    out_s = _out_proj(y, wo16, xs, rep(gate), m_s, 1024)
    return out_p, out_s, kp, vp, s_p, k32, v32, s_s


def kernel(x_prompt, x_sample, cache_k, cache_v, state_gla, page_table, c_prompt, c_sample,
           w_ada, b_ada, norm_g, w_in, w_gate_up, b_gate, gla_norm_g, q_norm_g, k_norm_g,
           sb_bias, w_branch_a, w_branch_b, w_out):
    depth = w_ada.shape[0]
    n_b, t_len, d = x_prompt.shape
    assert n_b == 1, "the prompt group is a single sequence"
    n_seq, n_tok, _ = x_sample.shape
    heads = d // SB_HEAD_DIM
    pad = (-(n_seq + n_b)) % 8
    c_all = jnp.concatenate([c_sample, c_prompt, jnp.zeros((pad, d), F32)], axis=0)
    xp = x_prompt.reshape(t_len, d)
    xs = x_sample.reshape(n_seq * n_tok, d)
    outs = [[] for _ in range(6)]
    for l in range(depth):
        w = (w_ada[l], b_ada[l], norm_g[l], w_in[l], w_gate_up[l], b_gate[l], gla_norm_g[l],
             q_norm_g[l], k_norm_g[l], sb_bias[l], w_branch_a[l], w_branch_b[l], w_out[l])
        xp, xs, kp, vp, sp, ks, vs, ss = _layer(xp, xs, c_all, w, cache_k[l], cache_v[l],
                                                 state_gla[l], page_table, n_seq, n_tok)
        outs[0].append(kp.reshape(n_b, t_len, heads, SB_HEAD_DIM))
        outs[1].append(vp.reshape(n_b, t_len, heads, SB_HEAD_DIM))
        outs[2].append(sp[None])
        outs[3].append(ks.reshape(n_seq, n_tok, heads, SB_HEAD_DIM))
        outs[4].append(vs.reshape(n_seq, n_tok, heads, SB_HEAD_DIM))
        outs[5].append(ss)
    k_p, v_p, s_p, k_s, v_s, s_s = (jnp.stack(o) for o in outs)
    return (xp.reshape(n_b, t_len, d), xs.reshape(n_seq, n_tok, d), k_p, v_p, s_p, k_s, v_s, s_s)
```
